```python
import functools
import jax, jax.numpy as jnp
from jax import lax
import numpy as np

D_MODEL = 4096
BATCH = 4
SEQ = 4096
DEPTH = 2

CTX_LEN = 256
GRID_W = 64
HEAD_DIM = 128
M_WIDTH = 3 * D_MODEL // 8
N_WIDTH = 3 * D_MODEL // 8
C_WIDTH = D_MODEL - M_WIDTH - N_WIDTH
M_HEADS = M_WIDTH // HEAD_DIM
N_HEADS = N_WIDTH // HEAD_DIM
M_PROJ = 4 * M_WIDTH + 4 * M_HEADS
D_IN = M_PROJ + 2 * C_WIDTH + 3 * N_WIDTH
MLSTM_CHUNK = 64
CONV_WIDTH = 31
NA_ROWS = 8
NA_COLS = 16
ROPE_BASE = 10000.0
D_FF = 2 * D_MODEL
N_EXPERTS = 8
TOP_K = 2
D_EXPERT = D_MODEL
N_MOD = 6
EPS = 1e-6

kernel_name = "hybrid_mlstm_conformer_natten_moe_dit"


def rmsnorm(x, g):
    xf = x.astype(jnp.float32)
    y = xf * lax.rsqrt(jnp.mean(xf * xf, axis=-1, keepdims=True) + EPS)
    return (y * g.astype(jnp.float32)).astype(x.dtype)


def modulate(x, shift, scale):
    return x * (1 + scale) + shift


def split_heads(a, n_heads):
    b, l, _ = a.shape
    return a.reshape(b, l, n_heads, HEAD_DIM).transpose(0, 2, 1, 3)


def merge_heads(a):
    b, h, l, d = a.shape
    return a.transpose(0, 2, 1, 3).reshape(b, l, h * d)


def axial_rope_tables(length):
    t = jnp.arange(length)
    n_freq = HEAD_DIM // 4
    freqs = ROPE_BASE ** (-jnp.arange(n_freq, dtype=jnp.float32) / n_freq)
    ang_r = (t // GRID_W).astype(jnp.float32)[:, None] * freqs
    ang_c = (t % GRID_W).astype(jnp.float32)[:, None] * freqs
    ang = jnp.concatenate([ang_r, ang_r, ang_c, ang_c], axis=-1)
    return jnp.cos(ang), jnp.sin(ang)


def apply_axial_rope(x, cos, sin):
    x1, x2, x3, x4 = jnp.split(x, 4, axis=-1)
    rot = jnp.concatenate([-x2, x1, -x4, x3], axis=-1)
    return x * cos + rot * sin


def mlstm_scan(q, k, v, log_i, log_f, state, with_out):
    b_, h_, l_, dh = k.shape
    nc = l_ // MLSTM_CHUNK

    def chunks(a):
        return jnp.moveaxis(a.reshape(b_, h_, nc, MLSTM_CHUNK, *a.shape[3:]), 2, 0)

    lower = jnp.tril(jnp.ones((MLSTM_CHUNK, MLSTM_CHUNK), dtype=bool))

    def body(carry, xs):
        C, n, m = carry
        kc, vc, ic, fc = xs[:4]
        bcum = jnp.cumsum(fc, axis=-1)
        b_last = bcum[..., -1]
        g = b_last[..., None] - bcum + ic
        m_new = jnp.maximum(b_last + m, jnp.max(g, axis=-1))
        decay = jnp.exp(b_last + m - m_new)
        w = jnp.exp(g - m_new[..., None])
        C_new = decay[..., None, None] * C + jnp.einsum('bhj,bhjk,bhjv->bhkv', w, kc, vc)
        n_new = decay[..., None] * n + jnp.einsum('bhj,bhjk->bhk', w, kc)
        if not with_out:
            return (C_new, n_new, m_new), None
        qc = xs[4]
        dmat = jnp.where(lower, bcum[..., :, None] - bcum[..., None, :] + ic[..., None, :], -jnp.inf)
        m_inter = bcum + m[..., None]
        m_t = jnp.maximum(m_inter, jnp.max(dmat, axis=-1))
        s = jnp.einsum('bhtd,bhjd->bhtj', qc, kc) * jnp.exp(dmat - m_t[..., None])
        inter = jnp.exp(m_inter - m_t)
        num = jnp.einsum('bhtj,bhjv->bhtv', s, vc) + inter[..., None] * jnp.einsum('bhtk,bhkv->bhtv', qc, C)
        den = jnp.sum(s, axis=-1) + inter * jnp.einsum('bhtk,bhk->bht', qc, n)
        h = num / jnp.maximum(jnp.abs(den), jnp.exp(-m_t))[..., None]
        return (C_new, n_new, m_new), h

    xs = (chunks(k), chunks(v), chunks(log_i), chunks(log_f))
    if with_out:
        xs = xs + (chunks(q),)
    state, hs = lax.scan(body, state, xs)
    if with_out:
        hs = jnp.moveaxis(hs, 0, 2).reshape(b_, h_, l_, dh)
    return hs, state


def mlstm_mixer(pm_ctx, pm_lat, gate_bias, norm_g, rope_cos, rope_sin, with_ctx_out):
    out_dtype = pm_lat.dtype

    def prep(pm):
        pm = pm.astype(jnp.float32)
        b_, l_ = pm.shape[:2]
        q, k, v, o, g = jnp.split(pm, [M_WIDTH, 2 * M_WIDTH, 3 * M_WIDTH, 4 * M_WIDTH], axis=-1)
        g = (g.reshape(b_, l_, 4, M_HEADS) + gate_bias.astype(jnp.float32)).transpose(2, 0, 3, 1)
        return split_heads(q, M_HEADS) * HEAD_DIM ** -0.5, split_heads(k, M_HEADS), split_heads(v, M_HEADS), o, g

    qc, kc, vc, oc, gc = prep(pm_ctx)
    ql, kl, vl, ol, gl = prep(pm_lat)
    ql = apply_axial_rope(ql, rope_cos, rope_sin)
    kl = apply_axial_rope(kl, rope_cos, rope_sin)
    b_ = pm_lat.shape[0]
    zero = (jnp.zeros((b_, M_HEADS, HEAD_DIM, HEAD_DIM), jnp.float32),
            jnp.zeros((b_, M_HEADS, HEAD_DIM), jnp.float32),
            jnp.zeros((b_, M_HEADS), jnp.float32))
    h_ctx_dirs, h_lat_dirs = [], []
    for d in range(2):
        fl = (lambda a: a[:, :, ::-1]) if d == 1 else (lambda a: a)
        hc, st = mlstm_scan(fl(qc), fl(kc), fl(vc), fl(gc[2 * d]), fl(jax.nn.log_sigmoid(gc[2 * d + 1])), zero, with_ctx_out)
        hl, _ = mlstm_scan(fl(ql), fl(kl), fl(vl), fl(gl[2 * d]), fl(jax.nn.log_sigmoid(gl[2 * d + 1])), st, True)
        h_lat_dirs.append(fl(hl))
        if with_ctx_out:
            h_ctx_dirs.append(fl(hc))

    def finish(h, o):
        h = rmsnorm(h, norm_g.reshape(M_HEADS, 1, HEAD_DIM))
        return (jax.nn.sigmoid(o) * merge_heads(h)).astype(out_dtype)

    y_lat = finish(h_lat_dirs[0] + h_lat_dirs[1], ol)
    y_ctx = finish(h_ctx_dirs[0] + h_ctx_dirs[1], oc) if with_ctx_out else None
    return y_ctx, y_lat


def conformer_conv(pc, conv_w, conv_b, norm_g):
    a, g = jnp.split(pc, 2, axis=-1)
    u = a * jax.nn.sigmoid(g)
    u = lax.conv_general_dilated(u, conv_w[:, None, :].astype(u.dtype), window_strides=(1,),
                                 padding=[(CONV_WIDTH // 2, CONV_WIDTH // 2)],
                                 dimension_numbers=('NWC', 'WIO', 'NWC'),
                                 feature_group_count=u.shape[-1]) + conv_b
    return jax.nn.silu(rmsnorm(u, norm_g))


def neighbourhood_attention(q, k, v, k_ctx, v_ctx, rpb):
    b_, h_, l_, dh = q.shape
    rows = l_ // GRID_W
    wr = min(NA_ROWS, rows)
    scale = dh ** -0.5
    qg = q.reshape(b_, h_, rows, GRID_W, dh)
    kg = k.reshape(b_, h_, rows, GRID_W, dh)
    vg = v.reshape(b_, h_, rows, GRID_W, dh)
    col = jnp.arange(GRID_W)
    col_start = jnp.clip(col - NA_COLS // 2, 0, GRID_W - NA_COLS)
    col_mask = (col[None, :] >= col_start[:, None]) & (col[None, :] < col_start[:, None] + NA_COLS)
    col_idx = jnp.clip(col[None, :] - col[:, None] + NA_COLS - 1, 0, 2 * NA_COLS - 2)

    def row_block(r):
        row_start = jnp.clip(r - wr // 2, 0, rows - wr)
        q_r = lax.dynamic_index_in_dim(qg, r, axis=2, keepdims=False)
        k_blk = lax.dynamic_slice_in_dim(kg, row_start, wr, axis=2)
        v_blk = lax.dynamic_slice_in_dim(vg, row_start, wr, axis=2)
        row_idx = row_start + jnp.arange(wr) - r + NA_ROWS - 1
        bias = rpb[:, row_idx[None, :, None], col_idx[:, None, :]]
        s_win = jnp.einsum('bhqd,bhrkd->bhqrk', q_r, k_blk).astype(jnp.float32) * scale + bias
        s_win = jnp.where(col_mask[:, None, :], s_win, -jnp.inf).reshape(b_, h_, GRID_W, wr * GRID_W)
        s_ctx = jnp.einsum('bhqd,bhcd->bhqc', q_r, k_ctx).astype(jnp.float32) * scale
        p = jax.nn.softmax(jnp.concatenate([s_win, s_ctx], axis=-1), axis=-1).astype(v.dtype)
        p_win = p[..., :wr * GRID_W].reshape(b_, h_, GRID_W, wr, GRID_W)
        p_ctx = p[..., wr * GRID_W:]
        return (jnp.einsum('bhqrk,bhrkd->bhqd', p_win, v_blk)
                + jnp.einsum('bhqc,bhcd->bhqd', p_ctx, v_ctx))

    out = lax.map(row_block, jnp.arange(rows))
    return jnp.moveaxis(out, 0, 2).reshape(b_, h_, l_, dh)


def context_attention(q, k, v):
    s = jnp.einsum('bhqd,bhkd->bhqk', q, k).astype(jnp.float32) * HEAD_DIM ** -0.5
    p = jax.nn.softmax(s, axis=-1).astype(v.dtype)
    return jnp.einsum('bhqk,bhkd->bhqd', p, v)


def na_mixer(pn_ctx, pn_lat, rpb, with_ctx_out):
    qc, kc, vc = [split_heads(a, N_HEADS) for a in jnp.split(pn_ctx, 3, axis=-1)]
    ql, kl, vl = [split_heads(a, N_HEADS) for a in jnp.split(pn_lat, 3, axis=-1)]
    y_lat = merge_heads(neighbourhood_attention(ql, kl, vl, kc, vc, rpb))
    y_ctx = merge_heads(context_attention(qc, kc, vc)) if with_ctx_out else None
    return y_ctx, y_lat


def swiglu(x, wg, wu, wd):
    return (jax.nn.silu(x @ wg) * (x @ wu)) @ wd


def moe_swiglu(x, w_router, wg, wu, wd):
    shp = x.shape
    xt = x.reshape(-1, shp[-1])
    logits = (xt @ w_router).astype(jnp.float32)
    top_val, top_idx = lax.top_k(logits, TOP_K)
    w = jax.nn.softmax(top_val, axis=-1)
    combine = jnp.sum(jax.nn.one_hot(top_idx, N_EXPERTS, dtype=jnp.float32) * w[..., None], axis=1)
    y = jnp.zeros_like(xt)
    for e in range(N_EXPERTS):
        y = y + combine[:, e:e + 1].astype(xt.dtype) * swiglu(xt, wg[e], wu[e], wd[e])
    return y.reshape(shp)


def hybrid_layer(h_ctx, h_lat, mod_ctx, mod_lat, g_norm, w_in, w_out, m_gate_bias, m_norm,
                 conv_w, conv_b, conv_norm, na_rpb, ffn, rope_cos, rope_sin, with_ctx_out):
    c_sh1, c_sc1, c_gt1, c_sh2, c_sc2, c_gt2 = jnp.split(mod_ctx, N_MOD, axis=-1)
    l_sh1, l_sc1, l_gt1, l_sh2, l_sc2, l_gt2 = jnp.split(mod_lat[:, None, :], N_MOD, axis=-1)
    u_ctx = modulate(rmsnorm(h_ctx, g_norm[0]), c_sh1, c_sc1)
    u_lat = modulate(rmsnorm(h_lat, g_norm[0]), l_sh1, l_sc1)
    cuts = [M_PROJ, M_PROJ + 2 * C_WIDTH]
    pm_c, pc_c, pn_c = jnp.split(u_ctx @ w_in, cuts, axis=-1)
    pm_l, pc_l, pn_l = jnp.split(u_lat @ w_in, cuts, axis=-1)
    m_ctx, m_lat = mlstm_mixer(pm_c, pm_l, m_gate_bias, m_norm, rope_cos, rope_sin, with_ctx_out)
    n_ctx, n_lat = na_mixer(pn_c, pn_l, na_rpb, with_ctx_out)
    conv_lat = conformer_conv(pc_l, conv_w, conv_b, conv_norm)
    mix_lat = jnp.concatenate([m_lat, conv_lat, n_lat], axis=-1) @ w_out
    h_lat = h_lat + l_gt1 * rmsnorm(mix_lat, g_norm[1])
    f_lat = ffn(modulate(rmsnorm(h_lat, g_norm[2]), l_sh2, l_sc2))
    h_lat = h_lat + l_gt2 * rmsnorm(f_lat, g_norm[3])
    if not with_ctx_out:
        return None, h_lat
    conv_ctx = conformer_conv(pc_c, conv_w, conv_b, conv_norm)
    mix_ctx = jnp.concatenate([m_ctx, conv_ctx, n_ctx], axis=-1) @ w_out
    h_ctx = h_ctx + c_gt1 * rmsnorm(mix_ctx, g_norm[1])
    f_ctx = ffn(modulate(rmsnorm(h_ctx, g_norm[2]), c_sh2, c_sc2))
    h_ctx = h_ctx + c_gt2 * rmsnorm(f_ctx, g_norm[3])
    return h_ctx, h_lat


def setup_inputs(seed: int = 0) -> dict:
    key = jax.random.key(seed)
    ks = jax.random.split(key, 24)
    f32 = jnp.float32
    n_dense = (DEPTH + 1) // 2
    n_moe = DEPTH // 2

    def nrm(k, shape, scale):
        return jax.random.normal(k, shape, f32) * scale

    i_bias = nrm(ks[9], (DEPTH, 2, M_HEADS), 0.1)
    f_bias = jax.random.uniform(ks[10], (DEPTH, 2, M_HEADS), f32, 3.0, 6.0)
    gate_bias = jnp.stack([i_bias[:, 0], f_bias[:, 0], i_bias[:, 1], f_bias[:, 1]], axis=1)
    return {
        "x": nrm(ks[0], (BATCH, SEQ, D_MODEL), 1.0),
        "c": nrm(ks[1], (BATCH, D_MODEL), 1.0),
        "ctx": nrm(ks[2], (BATCH, CTX_LEN, D_MODEL), 1.0),
        "c_ctx": nrm(ks[3], (D_MODEL,), 1.0),
        "w_mod": nrm(ks[4], (DEPTH, D_MODEL, N_MOD * D_MODEL), 0.3 * D_MODEL ** -0.5),
        "b_mod": nrm(ks[5], (DEPTH, N_MOD * D_MODEL), 0.02),
        "g_norm": 1.0 + nrm(ks[6], (DEPTH, 4, D_MODEL), 0.02),
        "w_in": nrm(ks[7], (DEPTH, D_MODEL, D_IN), D_MODEL ** -0.5),
        "w_out": nrm(ks[8], (DEPTH, D_MODEL, D_MODEL), D_MODEL ** -0.5),
        "mlstm_gate_bias": gate_bias,
        "mlstm_norm": 1.0 + nrm(ks[11], (DEPTH, M_WIDTH), 0.02),
        "conv_w": nrm(ks[12], (DEPTH, CONV_WIDTH, C_WIDTH), CONV_WIDTH ** -0.5),
        "conv_b": nrm(ks[13], (DEPTH, C_WIDTH), 0.02),
        "conv_norm": 1.0 + nrm(ks[14], (DEPTH, C_WIDTH), 0.02),
        "na_rpb": nrm(ks[15], (DEPTH, N_HEADS, 2 * NA_ROWS - 1, 2 * NA_COLS - 1), 0.1),
        "ffn_w_gate": nrm(ks[16], (n_dense, D_MODEL, D_FF), D_MODEL ** -0.5),
        "ffn_w_up": nrm(ks[17], (n_dense, D_MODEL, D_FF), D_MODEL ** -0.5),
        "ffn_w_down": nrm(ks[18], (n_dense, D_FF, D_MODEL), D_FF ** -0.5),
        "router_w": nrm(ks[19], (n_moe, D_MODEL, N_EXPERTS), D_MODEL ** -0.5),
        "moe_w_gate": nrm(ks[20], (n_moe, N_EXPERTS, D_MODEL, D_EXPERT), D_MODEL ** -0.5),
        "moe_w_up": nrm(ks[21], (n_moe, N_EXPERTS, D_MODEL, D_EXPERT), D_MODEL ** -0.5),
        "moe_w_down": nrm(ks[22], (n_moe, N_EXPERTS, D_EXPERT, D_MODEL), D_EXPERT ** -0.5),
    }


def reference(x, c, ctx, c_ctx, w_mod, b_mod, g_norm, w_in, w_out, mlstm_gate_bias, mlstm_norm,
              conv_w, conv_b, conv_norm, na_rpb, ffn_w_gate, ffn_w_up, ffn_w_down,
              router_w, moe_w_gate, moe_w_up, moe_w_down):
    rope_cos, rope_sin = axial_rope_tables(x.shape[1])
    s_lat = jax.nn.silu(c)
    s_ctx = jax.nn.silu(c_ctx)
    h_ctx, h_lat = ctx, x
    for layer in range(DEPTH):
        mod_lat = s_lat @ w_mod[layer] + b_mod[layer]
        mod_ctx = s_ctx @ w_mod[layer] + b_mod[layer]
        j = layer // 2
        if layer % 2 == 0:
            ffn = functools.partial(swiglu, wg=ffn_w_gate[j], wu=ffn_w_up[j], wd=ffn_w_down[j])
        else:
            ffn = functools.partial(moe_swiglu, w_router=router_w[j], wg=moe_w_gate[j],
                                    wu=moe_w_up[j], wd=moe_w_down[j])
        h_ctx, h_lat = hybrid_layer(h_ctx, h_lat, mod_ctx, mod_lat, g_norm[layer], w_in[layer], w_out[layer],
                                    mlstm_gate_bias[layer], mlstm_norm[layer], conv_w[layer], conv_b[layer],
                                    conv_norm[layer], na_rpb[layer], ffn, rope_cos, rope_sin,
                                    layer < DEPTH - 1)
    return h_lat
```

```python
import functools

import jax
import jax.numpy as jnp
from jax import lax
from jax.experimental import pallas as pl
from jax.experimental.pallas import tpu as pltpu

HEAD_DIM = 128
GRID_W = 64
NA_ROWS = 8
NA_COLS = 16
ROPE_BASE = 10000.0
N_MOD = 6
EPS = 1e-6
MLSTM_CHUNK = 256
ROW_TILE = 256
HALO = 16
LANES = 128
NEG = -1e30
VMEM_LIMIT = 56 * 1024 * 1024

F32 = jnp.float32
BF16 = jnp.bfloat16


def _params(sem):
    return pltpu.CompilerParams(dimension_semantics=sem, vmem_limit_bytes=VMEM_LIMIT)


def _tile(n, target, mult=LANES):
    if n <= target:
        return n
    for step in (mult, LANES, 8):
        for t in range((target // step) * step, 0, -step):
            if n % t == 0:
                return t
    raise ValueError((n, target, mult))


def _dot(a, b):
    return jnp.dot(a, b, preferred_element_type=F32)


def _dot_nt(a, b):
    return lax.dot_general(a, b, (((1,), (1,)), ((), ())), preferred_element_type=F32)


def _dot_tn(a, b):
    return lax.dot_general(a, b, (((0,), (0,)), ((), ())), preferred_element_type=F32)


def _sigmoid(x):
    return 1.0 / (1.0 + jnp.exp(-x))


def _rms(x, g):
    return x * lax.rsqrt(jnp.mean(x * x, axis=-1, keepdims=True) + EPS) * g


def _mod_body(c_ref, w_ref, b_ref, o_ref):
    k = pl.program_id(2)
    c = c_ref[...]
    s = (c * _sigmoid(c)).astype(BF16)
    part = _dot(s, w_ref[...].astype(BF16))

    @pl.when(k == 0)
    def _():
        o_ref[...] = part + b_ref[...]

    @pl.when(k > 0)
    def _():
        o_ref[...] += part


def _modulation(cond, w_mod, b_mod):
    depth, d, n = w_mod.shape
    tn, tk = _tile(n, 2048), _tile(d, 1024)
    return pl.pallas_call(
        _mod_body,
        out_shape=jax.ShapeDtypeStruct((depth, 8, n), F32),
        grid=(depth, n // tn, d // tk),
        in_specs=[pl.BlockSpec((8, tk), lambda l, j, k: (0, k)),
                  pl.BlockSpec((None, tk, tn), lambda l, j, k: (l, k, j)),
                  pl.BlockSpec((None, 1, tn), lambda l, j, k: (l, 0, j))],
        out_specs=pl.BlockSpec((None, 8, tn), lambda l, j, k: (l, 0, j)),
        compiler_params=_params(("parallel", "parallel", "arbitrary")),
        name="modulation",
    )(cond, w_mod, b_mod.reshape(depth, 1, n))


def _group_map(n_lat_tiles, lat_tiles_per_seq, n_batch):
    def group(i):
        return jnp.where(i < n_lat_tiles, i // lat_tiles_per_seq, n_batch + (i - n_lat_tiles))
    return group


def _normmod_body(h_ref, g_ref, sh_ref, sc_ref, u_ref):
    y = _rms(h_ref[...], g_ref[...])
    u_ref[...] = (y * (1.0 + sc_ref[...]) + sh_ref[...]).astype(u_ref.dtype)


def _normmod(h, g, sh, sc, group, n_rows):
    d = h.shape[1]
    row = pl.BlockSpec((ROW_TILE, d), lambda i: (i, 0))
    vec = pl.BlockSpec((1, d), lambda i: (0, 0))
    per_group = pl.BlockSpec((None, 1, d), lambda i: (group(i), 0, 0))
    return pl.pallas_call(
        _normmod_body,
        out_shape=jax.ShapeDtypeStruct((n_rows, d), BF16),
        grid=(n_rows // ROW_TILE,),
        in_specs=[row, vec, per_group, per_group],
        out_specs=row,
        compiler_params=_params(("parallel",)),
        name="normmod",
    )(h, g, sh, sc)


def _post_body(h_ref, y_ref, gp_ref, gate_ref, *rest, with_next):
    h = h_ref[...] + gate_ref[...] * _rms(y_ref[...], gp_ref[...])
    if with_next:
        gn_ref, sh_ref, sc_ref, ho_ref, u_ref = rest
        ho_ref[...] = h
        u_ref[...] = (_rms(h, gn_ref[...]) * (1.0 + sc_ref[...]) + sh_ref[...]).astype(u_ref.dtype)
    else:
        (ho_ref,) = rest
        ho_ref[...] = h


def _post(h, y, g_post, gate, group, n_rows, nxt=None):
    d = h.shape[1]
    row = pl.BlockSpec((ROW_TILE, d), lambda i: (i, 0))
    vec = pl.BlockSpec((1, d), lambda i: (0, 0))
    per_group = pl.BlockSpec((None, 1, d), lambda i: (group(i), 0, 0))
    in_specs = [row, row, vec, per_group]
    args = [h, y, g_post, gate]
    out_shape = [jax.ShapeDtypeStruct((n_rows, d), F32)]
    out_specs = [row]
    if nxt is not None:
        in_specs += [vec, per_group, per_group]
        args += list(nxt)
        out_shape.append(jax.ShapeDtypeStruct((n_rows, d), BF16))
        out_specs.append(row)
    return pl.pallas_call(
        functools.partial(_post_body, with_next=nxt is not None),
        out_shape=out_shape,
        grid=(n_rows // ROW_TILE,),
        in_specs=in_specs,
        out_specs=out_specs,
        compiler_params=_params(("parallel",)),
        name="post",
    )(*args)


def _mm_body(*refs, n_a, nk):
    a_refs, b_refs, o_ref = refs[:n_a], refs[n_a:2 * n_a], refs[2 * n_a]
    part = _dot(a_refs[0][...], b_refs[0][...])
    for a_ref, b_ref in zip(a_refs[1:], b_refs[1:]):
        part += _dot(a_ref[...], b_ref[...])
    if nk == 1:
        o_ref[...] = part.astype(o_ref.dtype)
        return
    acc_ref = refs[2 * n_a + 1]
    k = pl.program_id(2)

    @pl.when(k == 0)
    def _():
        acc_ref[...] = part

    @pl.when(k > 0)
    def _():
        acc_ref[...] += part

    @pl.when(k == nk - 1)
    def _():
        o_ref[...] = acc_ref[...].astype(o_ref.dtype)


def _matmul(a_list, b_list, out_dtype, m, tm, tn, tk=None, name="matmul"):
    n = b_list[0].shape[1]
    n_a = len(a_list)
    if tk is None:
        nk = 1
        in_specs = ([pl.BlockSpec((tm, a.shape[1]), lambda i, j: (i, 0)) for a in a_list]
                    + [pl.BlockSpec((b.shape[0], tn), lambda i, j: (0, j)) for b in b_list])
        grid = (m // tm, n // tn)
        out_spec = pl.BlockSpec((tm, tn), lambda i, j: (i, j))
        scratch = []
        sem = ("parallel", "parallel")
    else:
        assert n_a == 1
        kdim = a_list[0].shape[1]
        nk = kdim // tk
        in_specs = [pl.BlockSpec((tm, tk), lambda i, j, k: (i, k)),
                    pl.BlockSpec((tk, tn), lambda i, j, k: (k, j))]
        grid = (m // tm, n // tn, nk)
        out_spec = pl.BlockSpec((tm, tn), lambda i, j, k: (i, j))
        scratch = [pltpu.VMEM((tm, tn), F32)]
        sem = ("parallel", "parallel", "arbitrary")
    return pl.pallas_call(
        functools.partial(_mm_body, n_a=n_a, nk=nk),
        out_shape=jax.ShapeDtypeStruct((m, n), out_dtype),
        grid=grid,
        in_specs=in_specs,
        out_specs=out_spec,
        scratch_shapes=scratch,
        compiler_params=_params(sem),
        name=name,
    )(*a_list, *b_list)


def _glu_body(a_ref, wg_ref, wu_ref, o_ref):
    a = a_ref[...]
    g = _dot(a, wg_ref[...])
    u = _dot(a, wu_ref[...])
    o_ref[...] = (g * _sigmoid(g) * u).astype(o_ref.dtype)


def _glu(a, wg, wu, m, tm, tn):
    n_e, kdim, n = wg.shape
    nj = n // tn
    w_spec = pl.BlockSpec((None, kdim, tn), lambda i, j: (j // nj, 0, j % nj))
    return pl.pallas_call(
        _glu_body,
        out_shape=jax.ShapeDtypeStruct((m, n_e * n), BF16),
        grid=(m // tm, n_e * nj),
        in_specs=[pl.BlockSpec((tm, kdim), lambda i, j: (i, 0)), w_spec, w_spec],
        out_specs=pl.BlockSpec((tm, tn), lambda i, j: (i, j)),
        compiler_params=_params(("parallel", "parallel")),
        name="glu",
    )(a, wg, wu)


def _router_body(u_ref, w_ref, o_ref, *, n_experts):
    logits = _dot(u_ref[...], w_ref[...])
    lane = lax.broadcasted_iota(jnp.int32, logits.shape, 1)
    l1 = jnp.where(lane < n_experts, logits, NEG)
    m1 = jnp.max(l1, axis=-1, keepdims=True)
    i1 = jnp.min(jnp.where(l1 == m1, lane, LANES), axis=-1, keepdims=True)
    l2 = jnp.where(lane == i1, NEG, l1)
    m2 = jnp.max(l2, axis=-1, keepdims=True)
    i2 = jnp.min(jnp.where(l2 == m2, lane, LANES), axis=-1, keepdims=True)
    e2 = jnp.exp(m2 - m1)
    w1 = 1.0 / (1.0 + e2)
    w2 = e2 / (1.0 + e2)
    o_ref[...] = jnp.where(lane == i1, w1, 0.0) + jnp.where(lane == i2, w2, 0.0)


def _router(u, w_router_padded, m, n_experts):
    d = u.shape[1]
    tm = _tile(m, 1024, ROW_TILE)
    return pl.pallas_call(
        functools.partial(_router_body, n_experts=n_experts),
        out_shape=jax.ShapeDtypeStruct((m, LANES), F32),
        grid=(m // tm,),
        in_specs=[pl.BlockSpec((tm, d), lambda i: (i, 0)), pl.BlockSpec((d, LANES), lambda i: (0, 0))],
        out_specs=pl.BlockSpec((tm, LANES), lambda i: (i, 0)),
        compiler_params=_params(("parallel",)),
        name="router",
    )(u, w_router_padded)


def _moe_down_body(hid_ref, w_ref, comb_ref, o_ref):
    e = pl.program_id(2)
    acc = _dot(hid_ref[...], w_ref[...])
    comb = comb_ref[...]
    lane = lax.broadcasted_iota(jnp.int32, comb.shape, 1)
    c = jnp.sum(jnp.where(lane == e, comb, 0.0), axis=-1, keepdims=True)

    @pl.when(e == 0)
    def _():
        o_ref[...] = c * acc

    @pl.when(e > 0)
    def _():
        o_ref[...] += c * acc


def _moe_down(hid, wd, comb, m, tm, tn):
    n_e, kdim, n = wd.shape
    return pl.pallas_call(
        _moe_down_body,
        out_shape=jax.ShapeDtypeStruct((m, n), F32),
        grid=(m // tm, n // tn, n_e),
        in_specs=[pl.BlockSpec((tm, kdim), lambda i, j, e: (i, e)),
                  pl.BlockSpec((None, kdim, tn), lambda i, j, e: (e, 0, j)),
                  pl.BlockSpec((tm, LANES), lambda i, j, e: (i, 0))],
        out_specs=pl.BlockSpec((tm, tn), lambda i, j, e: (i, j)),
        compiler_params=_params(("parallel", "parallel", "arbitrary")),
        name="moe_down",
    )(hid, wd, comb)


def _split3(x):
    hi = x.astype(BF16)
    r1 = x - hi.astype(F32)
    mid = r1.astype(BF16)
    lo = (r1 - mid.astype(F32)).astype(BF16)
    return hi, mid, lo


def _gates_body(g_ref, b_ref, o_ref, *, n_heads):
    pre = g_ref[...] + b_ref[...]
    log_f = jnp.minimum(pre, 0.0) - jnp.log(1.0 + jnp.exp(-jnp.abs(pre)))
    lc = pre.shape[0]
    t = lax.broadcasted_iota(jnp.int32, (lc, lc), 0)
    s = lax.broadcasted_iota(jnp.int32, (lc, lc), 1)
    tri_f = jnp.where(s <= t, 1.0, 0.0).astype(BF16)
    tri_b = jnp.where(s >= t, 1.0, 0.0).astype(BF16)
    hi, mid, lo = _split3(log_f)
    cs_f = _dot(tri_f, hi) + _dot(tri_f, mid) + _dot(tri_f, lo)
    cs_b = _dot(tri_b, hi) + _dot(tri_b, mid) + _dot(tri_b, lo)
    lane = lax.broadcasted_iota(jnp.int32, pre.shape, 1)
    is_ff = (lane >= n_heads) & (lane < 2 * n_heads)
    is_fb = (lane >= 3 * n_heads) & (lane < 4 * n_heads)
    o_ref[...] = jnp.where(is_ff, cs_f, jnp.where(is_fb, cs_b, pre))


def _gates(g_raw, bias_row, n_heads):
    t_all = g_raw.shape[0]
    blk = pl.BlockSpec((MLSTM_CHUNK, LANES), lambda i: (i, 0))
    return pl.pallas_call(
        functools.partial(_gates_body, n_heads=n_heads),
        out_shape=jax.ShapeDtypeStruct((t_all, LANES), F32),
        grid=(t_all // MLSTM_CHUNK,),
        in_specs=[blk, pl.BlockSpec((1, LANES), lambda i: (0, 0))],
        out_specs=blk,
        compiler_params=_params(("parallel",)),
        name="mlstm_gates",
    )(g_raw, bias_row)


def _mlstm_chunk(q, k, v, gc, gr, state, d, rev):
    c_st, n_st, m_st = state
    lc = q.shape[0]
    li_c, bc_c = gc[:, 2 * d:2 * d + 1], gc[:, 2 * d + 1:2 * d + 2]
    li_r, bc_r = gr[2 * d:2 * d + 1, :], gr[2 * d + 1:2 * d + 2, :]
    b_last = bc_r[:, 0:1] if rev else bc_r[:, lc - 1:lc]
    g_c = b_last - bc_c + li_c
    g_r = b_last - bc_r + li_r
    m_new = jnp.maximum(b_last + m_st, jnp.max(g_r, axis=-1, keepdims=True))
    decay = jnp.exp(b_last + m_st - m_new)
    kw = k.astype(F32) * jnp.exp(g_c - m_new)
    c_new = decay * c_st + _dot_tn(kw.astype(BF16), v)
    n_new = decay * n_st + jnp.sum(kw, axis=0, keepdims=True)

    t = lax.broadcasted_iota(jnp.int32, (lc, lc), 0)
    j = lax.broadcasted_iota(jnp.int32, (lc, lc), 1)
    seen = (j >= t) if rev else (j <= t)
    dmat = jnp.where(seen, bc_c - bc_r + li_r, NEG)
    m_inter = bc_c + m_st
    m_t = jnp.maximum(m_inter, jnp.max(dmat, axis=-1, keepdims=True))
    s = _dot_nt(q, k) * jnp.exp(dmat - m_t)
    inter = jnp.exp(m_inter - m_t)
    num = _dot(s.astype(BF16), v) + inter * _dot(q, c_st.astype(BF16))
    den = jnp.sum(s, axis=-1, keepdims=True) + inter * jnp.sum(q.astype(F32) * n_st, axis=-1, keepdims=True)
    h = num / jnp.maximum(jnp.abs(den), jnp.exp(-m_t))
    return h, (c_new, n_new, m_new)


def _mlstm_body(qc_ref, kc_ref, vc_ref, oc_ref, ql_ref, kl_ref, vl_ref, ol_ref,
                gcc_ref, grc_ref, gcl_ref, grl_ref, cos_ref, sin_ref, ng_ref, *rest, with_ctx_out):
    if with_ctx_out:
        yl_ref, yc_ref, qs_ref, ks_ref, hf_ref, hb_ref = rest
    else:
        yl_ref, qs_ref, ks_ref, hf_ref, hb_ref = rest
    lc = MLSTM_CHUNK
    n_lat = ql_ref.shape[0] // lc
    n_ctx = qc_ref.shape[0] // lc
    scale = HEAD_DIM ** -0.5
    lane = lax.broadcasted_iota(jnp.int32, (lc, HEAD_DIM), 1)
    first_quarter = (lane % (HEAD_DIM // 2)) < (HEAD_DIM // 4)

    def rope(x, rows):
        partner = jnp.where(first_quarter, pltpu.roll(x, HEAD_DIM - HEAD_DIM // 4, 1), pltpu.roll(x, HEAD_DIM // 4, 1))
        return x * cos_ref[rows, :] + partner * sin_ref[rows, :]

    def rope_chunk(i, carry):
        rows = pl.ds(pl.multiple_of(i * lc, lc), lc)
        qs_ref[rows, :] = rope(ql_ref[rows, :].astype(F32) * scale, rows).astype(BF16)
        ks_ref[rows, :] = rope(kl_ref[rows, :].astype(F32), rows).astype(BF16)
        return carry

    lax.fori_loop(0, n_lat, rope_chunk, 0)

    ng = ng_ref[...]

    def finish(h, o_gate):
        return (_sigmoid(o_gate.astype(F32)) * _rms(h, ng)).astype(BF16)

    zero = (jnp.zeros((HEAD_DIM, HEAD_DIM), F32), jnp.zeros((1, HEAD_DIM), F32), jnp.zeros((1, 1), F32))

    st = [zero, zero]
    for i in range(n_ctx):
        for d in range(2):
            ci = i if d == 0 else n_ctx - 1 - i
            rows = pl.ds(ci * lc, lc)
            q = (qc_ref[rows, :].astype(F32) * scale).astype(BF16)
            h, st[d] = _mlstm_chunk(q, kc_ref[rows, :], vc_ref[rows, :], gcc_ref[rows, :], grc_ref[:, rows],
                                    st[d], d, d == 1)
            if with_ctx_out:
                (hf_ref if d == 0 else hb_ref)[rows, :] = h
    if with_ctx_out:
        for i in range(n_ctx):
            rows = pl.ds(i * lc, lc)
            yc_ref[rows, :] = finish(hf_ref[rows, :] + hb_ref[rows, :], oc_ref[rows, :])

    def lat_step(i, carry):
        st_f, st_b = carry
        out = []
        for d, st_d in ((0, st_f), (1, st_b)):
            ci = i if d == 0 else n_lat - 1 - i
            rows = pl.ds(pl.multiple_of(ci * lc, lc), lc)
            h, st_d = _mlstm_chunk(qs_ref[rows, :], ks_ref[rows, :], vl_ref[rows, :], gcl_ref[rows, :],
                                   grl_ref[:, rows], st_d, d, d == 1)
            (hf_ref if d == 0 else hb_ref)[rows, :] = h
            out.append(st_d)
        return tuple(out)

    lax.fori_loop(0, n_lat, lat_step, (st[0], st[1]))

    def finish_chunk(i, carry):
        rows = pl.ds(pl.multiple_of(i * lc, lc), lc)
        yl_ref[rows, :] = finish(hf_ref[rows, :] + hb_ref[rows, :], ol_ref[rows, :])
        return carry

    lax.fori_loop(0, n_lat, finish_chunk, 0)


def _mlstm(p, g_col, g_row, cos, sin_signed, norm_g, n_batch, seq, ctx_len, n_heads, with_ctx_out):
    ctx_blk0 = n_batch * seq // ctx_len

    def col(kind, lat):
        rows = seq if lat else ctx_len
        if lat:
            return pl.BlockSpec((rows, HEAD_DIM), lambda b, h: (b, kind * n_heads + h))
        return pl.BlockSpec((rows, HEAD_DIM), lambda b, h: (ctx_blk0 + b, kind * n_heads + h))

    in_specs = ([col(kind, False) for kind in range(4)] + [col(kind, True) for kind in range(4)] + [
        pl.BlockSpec((None, ctx_len, 4), lambda b, h: (h, ctx_blk0 + b, 0)),
        pl.BlockSpec((None, 4, ctx_len), lambda b, h: (h, 0, ctx_blk0 + b)),
        pl.BlockSpec((None, seq, 4), lambda b, h: (h, b, 0)),
        pl.BlockSpec((None, 4, seq), lambda b, h: (h, 0, b)),
        pl.BlockSpec((seq, HEAD_DIM), lambda b, h: (0, 0)),
        pl.BlockSpec((seq, HEAD_DIM), lambda b, h: (0, 0)),
        pl.BlockSpec((None, 1, HEAD_DIM), lambda b, h: (h, 0, 0)),
    ])
    out_shape = [jax.ShapeDtypeStruct((n_batch * seq, n_heads * HEAD_DIM), BF16)]
    out_specs = [pl.BlockSpec((seq, HEAD_DIM), lambda b, h: (b, h))]
    if with_ctx_out:
        out_shape.append(jax.ShapeDtypeStruct((n_batch * ctx_len, n_heads * HEAD_DIM), BF16))
        out_specs.append(pl.BlockSpec((ctx_len, HEAD_DIM), lambda b, h: (b, h)))
    return pl.pallas_call(
        functools.partial(_mlstm_body, with_ctx_out=with_ctx_out),
        out_shape=out_shape,
        grid=(n_batch, n_heads),
        in_specs=in_specs,
        out_specs=out_specs,
        scratch_shapes=[pltpu.VMEM((seq, HEAD_DIM), BF16), pltpu.VMEM((seq, HEAD_DIM), BF16),
                        pltpu.VMEM((seq, HEAD_DIM), F32), pltpu.VMEM((seq, HEAD_DIM), F32)],
        compiler_params=_params(("parallel", "parallel")),
        name="mlstm",
    )(p, p, p, p, p, p, p, p, g_col, g_row, g_col, g_row, cos, sin_signed, norm_g)


def _conv_body(ap_ref, gp_ref, a_ref, g_ref, an_ref, gn_ref, w_ref, b_ref, ng_ref, o_ref, u_ref, c_ref,
               *, n_lat_tiles, lat_tiles_per_seq, width):
    i = pl.program_id(0)
    is_lat = i < n_lat_tiles
    pos = i % lat_tiles_per_seq
    first = jnp.logical_or(jnp.logical_not(is_lat), pos == 0)
    last = jnp.logical_or(jnp.logical_not(is_lat), pos == lat_tiles_per_seq - 1)

    def glu(x_ref, y_ref):
        return x_ref[...].astype(F32) * _sigmoid(y_ref[...].astype(F32))

    tile = a_ref.shape[0]
    u_ref[pl.ds(0, HALO), :] = jnp.where(first, 0.0, 1.0) * glu(ap_ref, gp_ref)
    u_ref[pl.ds(HALO, tile), :] = glu(a_ref, g_ref)
    u_ref[pl.ds(HALO + tile, HALO), :] = jnp.where(last, 0.0, 1.0) * glu(an_ref, gn_ref)

    strip, cblk = 32, 256
    n_ch = a_ref.shape[1]
    cblk = min(cblk, n_ch)
    base = HALO - width // 2

    for r0 in range(0, tile, strip):
        for cb in range(n_ch // cblk):
            cols = pl.ds(cb * cblk, cblk)
            acc = jnp.zeros((strip, cblk), F32)
            for k in range(width):
                acc += w_ref[pl.ds(k, 1), cols] * u_ref[pl.ds(r0 + base + k, strip), cols]
            c_ref[pl.ds(r0, strip), cols] = acc
    y = _rms(c_ref[...] + b_ref[...], ng_ref[...])
    o_ref[...] = (y * _sigmoid(y)).astype(o_ref.dtype)


def _conv(p, conv_w, conv_b, conv_norm, a_col_blk, n_rows, n_lat_tiles, lat_tiles_per_seq):
    width, n_ch = conv_w.shape
    assert width // 2 <= HALO
    per = ROW_TILE // HALO
    last_halo = p.shape[0] // HALO - 1

    def cur(c):
        return pl.BlockSpec((ROW_TILE, n_ch), lambda i: (i, c))

    def prev(c):
        return pl.BlockSpec((HALO, n_ch), lambda i: (jnp.maximum(i * per - 1, 0), c))

    def nxt(c):
        return pl.BlockSpec((HALO, n_ch), lambda i: (jnp.minimum((i + 1) * per, last_halo), c))

    vec = pl.BlockSpec((1, n_ch), lambda i: (0, 0))
    a, g = a_col_blk, a_col_blk + 1
    return pl.pallas_call(
        functools.partial(_conv_body, n_lat_tiles=n_lat_tiles, lat_tiles_per_seq=lat_tiles_per_seq, width=width),
        out_shape=jax.ShapeDtypeStruct((n_rows, n_ch), BF16),
        grid=(n_rows // ROW_TILE,),
        in_specs=[prev(a), prev(g), cur(a), cur(g), nxt(a), nxt(g),
                  pl.BlockSpec((width, n_ch), lambda i: (0, 0)), vec, vec],
        out_specs=pl.BlockSpec((ROW_TILE, n_ch), lambda i: (i, 0)),
        scratch_shapes=[pltpu.VMEM((ROW_TILE + 2 * HALO, n_ch), F32), pltpu.VMEM((ROW_TILE, n_ch), F32)],
        compiler_params=_params(("parallel",)),
        name="conformer_conv",
    )(p, p, p, p, p, p, conv_w, conv_b, conv_norm)


def _na_body(*refs, rows, with_ctx_out):
    if with_ctx_out:
        q_ref, k_ref, v_ref, kc_ref, vc_ref, bias_ref, qc_ref, o_ref, oc_ref = refs
    else:
        q_ref, k_ref, v_ref, kc_ref, vc_ref, bias_ref, o_ref = refs
    scale = HEAD_DIM ** -0.5
    wr = NA_ROWS
    kc = kc_ref[...]
    vc = vc_ref[...]

    def row_block(r, carry):
        row_start = jnp.clip(r - wr // 2, 0, rows - wr)
        shift = row_start - r + NA_ROWS // 2
        q_r = q_ref[pl.ds(pl.multiple_of(r * GRID_W, GRID_W), GRID_W), :]
        win = pl.ds(pl.multiple_of(row_start * GRID_W, GRID_W), wr * GRID_W)
        s_win = _dot_nt(q_r, k_ref[win, :]) * scale + bias_ref[shift + NA_ROWS // 2 - 1]
        s_ctx = _dot_nt(q_r, kc) * scale
        m = jnp.maximum(jnp.max(s_win, axis=-1, keepdims=True), jnp.max(s_ctx, axis=-1, keepdims=True))
        p_win = jnp.exp(s_win - m)
        p_ctx = jnp.exp(s_ctx - m)
        denom = jnp.sum(p_win, axis=-1, keepdims=True) + jnp.sum(p_ctx, axis=-1, keepdims=True)
        out = (_dot(p_win.astype(BF16), v_ref[win, :]) + _dot(p_ctx.astype(BF16), vc)) / denom
        o_ref[pl.ds(pl.multiple_of(r * GRID_W, GRID_W), GRID_W), :] = out.astype(o_ref.dtype)
        return carry

    lax.fori_loop(0, rows, row_block, 0)

    if with_ctx_out:
        s = _dot_nt(qc_ref[...], kc) * scale
        p = jnp.exp(s - jnp.max(s, axis=-1, keepdims=True))
        oc_ref[...] = (_dot(p.astype(BF16), vc) / jnp.sum(p, axis=-1, keepdims=True)).astype(oc_ref.dtype)


def _na(p, bias, col0, n_batch, seq, ctx_len, n_heads, with_ctx_out):
    ctx_blk0 = n_batch * seq // ctx_len
    rows = seq // GRID_W

    def lat(kind):
        return pl.BlockSpec((seq, HEAD_DIM), lambda b, h: (b, col0 + kind * n_heads + h))

    def ctx(kind):
        return pl.BlockSpec((ctx_len, HEAD_DIM), lambda b, h: (ctx_blk0 + b, col0 + kind * n_heads + h))

    in_specs = [lat(0), lat(1), lat(2), ctx(1), ctx(2),
                pl.BlockSpec((None, NA_ROWS, GRID_W, NA_ROWS * GRID_W), lambda b, h: (h, 0, 0, 0))]
    args = [p, p, p, p, p, bias]
    out_shape = [jax.ShapeDtypeStruct((n_batch * seq, n_heads * HEAD_DIM), BF16)]
    out_specs = [pl.BlockSpec((seq, HEAD_DIM), lambda b, h: (b, h))]
    if with_ctx_out:
        in_specs.append(ctx(0))
        args.append(p)
        out_shape.append(jax.ShapeDtypeStruct((n_batch * ctx_len, n_heads * HEAD_DIM), BF16))
        out_specs.append(pl.BlockSpec((ctx_len, HEAD_DIM), lambda b, h: (b, h)))
    return pl.pallas_call(
        functools.partial(_na_body, rows=rows, with_ctx_out=with_ctx_out),
        out_shape=out_shape,
        grid=(n_batch, n_heads),
        in_specs=in_specs,
        out_specs=out_specs,
        compiler_params=_params(("parallel", "parallel")),
        name="neighbourhood_attention",
    )(*args)


def _na_bias_table(rpb, rows):
    wr = min(NA_ROWS, rows)
    col = jnp.arange(GRID_W)
    col_start = jnp.clip(col - NA_COLS // 2, 0, GRID_W - NA_COLS)
    col_mask = (col[None, :] >= col_start[:, None]) & (col[None, :] < col_start[:, None] + NA_COLS)
    col_idx = jnp.clip(col[None, :] - col[:, None] + NA_COLS - 1, 0, 2 * NA_COLS - 2)
    row_idx = jnp.arange(NA_ROWS)[:, None] + jnp.arange(wr)[None, :]
    bias = rpb[:, row_idx[:, None, :, None], col_idx[None, :, None, :]]
    bias = jnp.where(col_mask[None, None, :, None, :], bias, NEG)
    return bias.reshape(rpb.shape[0], NA_ROWS, GRID_W, wr * GRID_W)


def _rope_tables(length):
    t = jnp.arange(length)
    n_freq = HEAD_DIM // 4
    freqs = ROPE_BASE ** (-jnp.arange(n_freq, dtype=F32) / n_freq)
    ang_r = (t // GRID_W).astype(F32)[:, None] * freqs
    ang_c = (t % GRID_W).astype(F32)[:, None] * freqs
    ang = jnp.concatenate([ang_r, ang_r, ang_c, ang_c], axis=-1)
    sign = jnp.concatenate([-jnp.ones(n_freq), jnp.ones(n_freq), -jnp.ones(n_freq), jnp.ones(n_freq)])
    return jnp.cos(ang), jnp.sin(ang) * sign


def kernel(x, c, ctx, c_ctx, w_mod, b_mod, g_norm, w_in, w_out, mlstm_gate_bias, mlstm_norm, conv_w, conv_b,
           conv_norm, na_rpb, ffn_w_gate, ffn_w_up, ffn_w_down, router_w, moe_w_gate, moe_w_up, moe_w_down):
    n_batch, seq, d = x.shape
    ctx_len = ctx.shape[1]
    depth = w_mod.shape[0]
    m_width = mlstm_norm.shape[1]
    c_width = conv_w.shape[2]
    n_width = d - m_width - c_width
    mh, nh = m_width // HEAD_DIM, n_width // HEAD_DIM
    n_experts = router_w.shape[2]
    t_lat, t_ctx = n_batch * seq, n_batch * ctx_len
    t_all = t_lat + t_ctx
    assert seq % MLSTM_CHUNK == 0 and ctx_len % MLSTM_CHUNK == 0 and ctx_len == ROW_TILE
    assert t_lat % ctx_len == 0 and (4 * m_width) % c_width == 0 and 4 * mh <= LANES
    n_lat_tiles, lat_tiles_per_seq = t_lat // ROW_TILE, seq // ROW_TILE
    group = _group_map(n_lat_tiles, lat_tiles_per_seq, n_batch)

    h = jnp.concatenate([x.reshape(t_lat, d), ctx.reshape(t_ctx, d)], axis=0)
    cond = jnp.concatenate([c, jnp.broadcast_to(c_ctx[None], (8 - n_batch, d))], axis=0)
    mod = _modulation(cond, w_mod, b_mod)
    grp_rows = jnp.concatenate([jnp.arange(n_batch), jnp.full((n_batch,), n_batch)])
    cos, sin_signed = _rope_tables(seq)
    tm_all, tm_lat = _tile(t_all, 1024, ROW_TILE), _tile(t_lat, 1024, ROW_TILE)

    def mods(layer):
        m6 = mod[layer][grp_rows].reshape(2 * n_batch, N_MOD, 1, d)
        return [m6[:, k] for k in range(N_MOD)]

    gv = lambda layer, k: g_norm[layer, k].reshape(1, d)

    sh1, sc1, _, _, _, _ = mods(0)
    u = _normmod(h, gv(0, 0), sh1, sc1, group, t_all)

    for layer in range(depth):
        last = layer == depth - 1
        with_ctx_out = not last
        sh1, sc1, gt1, sh2, sc2, gt2 = mods(layer)
        rows_out = t_lat if last else t_all
        tm_out = tm_lat if last else tm_all

        w = w_in[layer]
        cut_g, cut_c, cut_n = 4 * m_width, 4 * m_width + 4 * mh, 4 * m_width + 4 * mh + 2 * c_width
        w_main = jnp.concatenate([w[:, :cut_g], w[:, cut_c:]], axis=1).astype(BF16)
        w_gate = jnp.pad(w[:, cut_g:cut_c], ((0, 0), (0, LANES - 4 * mh))).astype(BF16)
        n_main = w_main.shape[1]
        p = _matmul([u], [w_main], BF16, t_all, tm_all, _tile(n_main, 1280, 2 * LANES), name="w_in")
        g_raw = _matmul([u], [w_gate], F32, t_all, tm_all, LANES, name="w_in_gates")

        bias_row = jnp.pad(mlstm_gate_bias[layer].reshape(1, 4 * mh), ((0, 0), (0, LANES - 4 * mh)))
        g_tab = _gates(g_raw, bias_row, mh)[:, :4 * mh].reshape(t_all, 4, mh)
        g_col, g_row = g_tab.transpose(2, 0, 1), g_tab.transpose(2, 1, 0)
        ym = _mlstm(p, g_col, g_row, cos, sin_signed, mlstm_norm[layer].reshape(mh, 1, HEAD_DIM),
                    n_batch, seq, ctx_len, mh, with_ctx_out)
        yc = _conv(p, conv_w[layer], conv_b[layer].reshape(1, c_width), conv_norm[layer].reshape(1, c_width),
                   cut_g // c_width, rows_out, n_lat_tiles, lat_tiles_per_seq)
        bias = _na_bias_table(na_rpb[layer], seq // GRID_W)
        yn = _na(p, bias, (cut_g + 2 * c_width) // HEAD_DIM, n_batch, seq, ctx_len, nh, with_ctx_out)
        if with_ctx_out:
            ym, yn = jnp.concatenate(ym, axis=0), jnp.concatenate(yn, axis=0)
        else:
            ym, yn = ym[0], yn[0]

        wo = w_out[layer]
        wo_parts = [wo[:m_width].astype(BF16), wo[m_width:m_width + c_width].astype(BF16),
                    wo[m_width + c_width:].astype(BF16)]
        mix = _matmul([ym, yc, yn], wo_parts, F32, rows_out, tm_out, _tile(d, 1024, 2 * LANES), name="w_out")
        h, u2 = _post(h, mix, gv(layer, 1), gt1, group, rows_out, nxt=(gv(layer, 2), sh2, sc2))

        j = layer // 2
        if layer % 2 == 0:
            d_ff = ffn_w_gate.shape[2]
            hid = _glu(u2, ffn_w_gate[j][None].astype(BF16), ffn_w_up[j][None].astype(BF16),
                       rows_out, tm_out, _tile(d_ff, 512, 2 * LANES))
            f = _matmul([hid], [ffn_w_down[j].astype(BF16)], F32, rows_out, tm_out, _tile(d, 1024, 2 * LANES),
                        tk=_tile(d_ff, 2048, 2 * LANES), name="ffn_down")
        else:
            d_e = moe_w_gate.shape[3]
            w_r = jnp.pad(router_w[j], ((0, 0), (0, LANES - n_experts))).astype(BF16)
            comb = _router(u2, w_r, rows_out, n_experts)
            hid = _glu(u2, moe_w_gate[j].astype(BF16), moe_w_up[j].astype(BF16),
                       rows_out, tm_out, _tile(d_e, 512, 2 * LANES))
            f = _moe_down(hid, moe_w_down[j].astype(BF16), comb, rows_out, tm_out, _tile(d, 1024, 2 * LANES))

        if last:
            (h,) = _post(h, f, gv(layer, 3), gt2, group, rows_out)
        else:
            nsh1, nsc1 = mods(layer + 1)[:2]
            h, u = _post(h, f, gv(layer, 3), gt2, group, rows_out, nxt=(gv(layer + 1, 0), nsh1, nsc1))

    return h[:t_lat].reshape(n_batch, seq, d)
```

```python
import functools

import jax
import jax.numpy as jnp
from jax import lax
from jax.experimental import pallas as pl
from jax.experimental.pallas import tpu as pltpu

HEAD_DIM = 128
GRID_W = 64
NA_ROWS = 8
NA_COLS = 16
ROPE_BASE = 10000.0
N_MOD = 6
TOP_K = 2
EPS = 1e-6
MLSTM_CHUNK = 256
ROW_TILE = 256
HALO = 16
LANES = 128
NEG = -1e30
VMEM_LIMIT = 56 * 1024 * 1024

F32 = jnp.float32
BF16 = jnp.bfloat16
U32 = jnp.uint32
MOE_TILE = 256
RANK_TILE = 512
DMA_CHUNK = 128


def _params(sem):
    return pltpu.CompilerParams(dimension_semantics=sem, vmem_limit_bytes=VMEM_LIMIT)


def _tile(n, target, mult=LANES):
    if n <= target:
        return n
    for step in (mult, LANES, 8):
        for t in range((target // step) * step, 0, -step):
            if n % t == 0:
                return t
    raise ValueError((n, target, mult))


def _dot(a, b):
    return jnp.dot(a, b, preferred_element_type=F32)


def _dot_nt(a, b):
    return lax.dot_general(a, b, (((1,), (1,)), ((), ())), preferred_element_type=F32)


def _dot_tn(a, b):
    return lax.dot_general(a, b, (((0,), (0,)), ((), ())), preferred_element_type=F32)


def _sigmoid(x):
    return 1.0 / (1.0 + jnp.exp(-x))


def _rms(x, g):
    return x * lax.rsqrt(jnp.mean(x * x, axis=-1, keepdims=True) + EPS) * g


def _mod_body(c_ref, w_ref, b_ref, o_ref):
    k = pl.program_id(2)
    c = c_ref[...]
    s = (c * _sigmoid(c)).astype(BF16)
    part = _dot(s, w_ref[...].astype(BF16))

    @pl.when(k == 0)
    def _():
        o_ref[...] = part + b_ref[...]

    @pl.when(k > 0)
    def _():
        o_ref[...] += part


def _modulation(cond, w_mod, b_mod):
    depth, d, n = w_mod.shape
    tn, tk = _tile(n, 2048), _tile(d, 1024)
    return pl.pallas_call(
        _mod_body,
        out_shape=jax.ShapeDtypeStruct((depth, 8, n), F32),
        grid=(depth, n // tn, d // tk),
        in_specs=[pl.BlockSpec((8, tk), lambda l, j, k: (0, k)),
                  pl.BlockSpec((None, tk, tn), lambda l, j, k: (l, k, j)),
                  pl.BlockSpec((None, 1, tn), lambda l, j, k: (l, 0, j))],
        out_specs=pl.BlockSpec((None, 8, tn), lambda l, j, k: (l, 0, j)),
        compiler_params=_params(("parallel", "parallel", "arbitrary")),
        name="modulation",
    )(cond, w_mod, b_mod.reshape(depth, 1, n))


def _group_map(n_lat_tiles, lat_tiles_per_seq, n_batch):
    def group(i):
        return jnp.where(i < n_lat_tiles, i // lat_tiles_per_seq, n_batch + (i - n_lat_tiles))
    return group


def _normmod_body(h_ref, g_ref, sh_ref, sc_ref, u_ref):
    y = _rms(h_ref[...], g_ref[...])
    u_ref[...] = (y * (1.0 + sc_ref[...]) + sh_ref[...]).astype(u_ref.dtype)


def _normmod(h, g, sh, sc, group, n_rows):
    d = h.shape[1]
    row = pl.BlockSpec((ROW_TILE, d), lambda i: (i, 0))
    vec = pl.BlockSpec((1, d), lambda i: (0, 0))
    per_group = pl.BlockSpec((None, 1, d), lambda i: (group(i), 0, 0))
    return pl.pallas_call(
        _normmod_body,
        out_shape=jax.ShapeDtypeStruct((n_rows, d), BF16),
        grid=(n_rows // ROW_TILE,),
        in_specs=[row, vec, per_group, per_group],
        out_specs=row,
        compiler_params=_params(("parallel",)),
        name="normmod",
    )(h, g, sh, sc)


def _pack_halves(x):
    half = x.shape[1] // 2
    bits = lax.bitcast_convert_type(x.astype(BF16).astype(F32), U32)
    return (bits[:, half:] & jnp.uint32(0xFFFF0000)) | lax.shift_right_logical(bits[:, :half], jnp.uint32(16))


def _unpack_halves(w):
    lo = lax.bitcast_convert_type(lax.shift_left(w, jnp.uint32(16)), F32).astype(BF16)
    hi = lax.bitcast_convert_type(w & jnp.uint32(0xFFFF0000), F32).astype(BF16)
    return lo, hi


def _post_body(h_ref, y_ref, gp_ref, gate_ref, *rest, with_next, pack_next):
    h = h_ref[...] + gate_ref[...] * _rms(y_ref[...], gp_ref[...])
    if with_next:
        gn_ref, sh_ref, sc_ref, ho_ref, u_ref = rest
        ho_ref[...] = h
        u = _rms(h, gn_ref[...]) * (1.0 + sc_ref[...]) + sh_ref[...]
        u_ref[...] = _pack_halves(u) if pack_next else u.astype(u_ref.dtype)
    else:
        (ho_ref,) = rest
        ho_ref[...] = h


def _post(h, y, g_post, gate, group, n_rows, nxt=None, pack_next=False):
    d = h.shape[1]
    row = pl.BlockSpec((ROW_TILE, d), lambda i: (i, 0))
    vec = pl.BlockSpec((1, d), lambda i: (0, 0))
    per_group = pl.BlockSpec((None, 1, d), lambda i: (group(i), 0, 0))
    in_specs = [row, row, vec, per_group]
    args = [h, y, g_post, gate]
    out_shape = [jax.ShapeDtypeStruct((n_rows, d), F32)]
    out_specs = [row]
    if nxt is not None:
        in_specs += [vec, per_group, per_group]
        args += list(nxt)
        if pack_next:
            out_shape.append(jax.ShapeDtypeStruct((n_rows, d // 2), U32))
            out_specs.append(pl.BlockSpec((ROW_TILE, d // 2), lambda i: (i, 0)))
        else:
            out_shape.append(jax.ShapeDtypeStruct((n_rows, d), BF16))
            out_specs.append(row)
    return pl.pallas_call(
        functools.partial(_post_body, with_next=nxt is not None, pack_next=pack_next),
        out_shape=out_shape,
        grid=(n_rows // ROW_TILE,),
        in_specs=in_specs,
        out_specs=out_specs,
        compiler_params=_params(("parallel",)),
        name="post",
    )(*args)


def _mm_body(*refs, n_a, nk):
    a_refs, b_refs, o_ref = refs[:n_a], refs[n_a:2 * n_a], refs[2 * n_a]
    part = _dot(a_refs[0][...], b_refs[0][...])
    for a_ref, b_ref in zip(a_refs[1:], b_refs[1:]):
        part += _dot(a_ref[...], b_ref[...])
    if nk == 1:
        o_ref[...] = part.astype(o_ref.dtype)
        return
    acc_ref = refs[2 * n_a + 1]
    k = pl.program_id(2)

    @pl.when(k == 0)
    def _():
        acc_ref[...] = part

    @pl.when(k > 0)
    def _():
        acc_ref[...] += part

    @pl.when(k == nk - 1)
    def _():
        o_ref[...] = acc_ref[...].astype(o_ref.dtype)


def _matmul(a_list, b_list, out_dtype, m, tm, tn, tk=None, name="matmul"):
    n = b_list[0].shape[1]
    n_a = len(a_list)
    if tk is None:
        nk = 1
        in_specs = ([pl.BlockSpec((tm, a.shape[1]), lambda i, j: (i, 0)) for a in a_list]
                    + [pl.BlockSpec((b.shape[0], tn), lambda i, j: (0, j)) for b in b_list])
        grid = (m // tm, n // tn)
        out_spec = pl.BlockSpec((tm, tn), lambda i, j: (i, j))
        scratch = []
        sem = ("parallel", "parallel")
    else:
        assert n_a == 1
        kdim = a_list[0].shape[1]
        nk = kdim // tk
        in_specs = [pl.BlockSpec((tm, tk), lambda i, j, k: (i, k)),
                    pl.BlockSpec((tk, tn), lambda i, j, k: (k, j))]
        grid = (m // tm, n // tn, nk)
        out_spec = pl.BlockSpec((tm, tn), lambda i, j, k: (i, j))
        scratch = [pltpu.VMEM((tm, tn), F32)]
        sem = ("parallel", "parallel", "arbitrary")
    return pl.pallas_call(
        functools.partial(_mm_body, n_a=n_a, nk=nk),
        out_shape=jax.ShapeDtypeStruct((m, n), out_dtype),
        grid=grid,
        in_specs=in_specs,
        out_specs=out_spec,
        scratch_shapes=scratch,
        compiler_params=_params(sem),
        name=name,
    )(*a_list, *b_list)


def _glu_body(a_ref, wg_ref, wu_ref, o_ref):
    a = a_ref[...]
    g = _dot(a, wg_ref[...])
    u = _dot(a, wu_ref[...])
    o_ref[...] = (g * _sigmoid(g) * u).astype(o_ref.dtype)


def _glu(a, wg, wu, m, tm, tn):
    n_e, kdim, n = wg.shape
    nj = n // tn
    w_spec = pl.BlockSpec((None, kdim, tn), lambda i, j: (j // nj, 0, j % nj))
    return pl.pallas_call(
        _glu_body,
        out_shape=jax.ShapeDtypeStruct((m, n_e * n), BF16),
        grid=(m // tm, n_e * nj),
        in_specs=[pl.BlockSpec((tm, kdim), lambda i, j: (i, 0)), w_spec, w_spec],
        out_specs=pl.BlockSpec((tm, tn), lambda i, j: (i, j)),
        compiler_params=_params(("parallel", "parallel")),
        name="glu",
    )(a, wg, wu)


def _router_body(u_ref, w_ref, o_ref, sel_ref, *, n_experts):
    lo, hi = _unpack_halves(u_ref[...])
    half = lo.shape[1]
    logits = _dot(lo, w_ref[pl.ds(0, half), :]) + _dot(hi, w_ref[pl.ds(half, half), :])
    lane = lax.broadcasted_iota(jnp.int32, logits.shape, 1)
    l1 = jnp.where(lane < n_experts, logits, NEG)
    m1 = jnp.max(l1, axis=-1, keepdims=True)
    i1 = jnp.min(jnp.where(l1 == m1, lane, LANES), axis=-1, keepdims=True)
    l2 = jnp.where(lane == i1, NEG, l1)
    m2 = jnp.max(l2, axis=-1, keepdims=True)
    i2 = jnp.min(jnp.where(l2 == m2, lane, LANES), axis=-1, keepdims=True)
    e2 = jnp.exp(m2 - m1)
    w1 = 1.0 / (1.0 + e2)
    w2 = e2 / (1.0 + e2)
    is1, is2 = lane == i1, lane == i2
    o_ref[...] = jnp.where(is1, w1, 0.0) + jnp.where(is2, w2, 0.0)
    sel_ref[...] = jnp.where(is1, 1.0, jnp.where(is2, 1.0, 0.0)).astype(sel_ref.dtype)


def _router(u_packed, w_router_padded, m, n_experts):
    half = u_packed.shape[1]
    tm = _tile(m, 1024, ROW_TILE)
    out = pl.BlockSpec((tm, LANES), lambda i: (i, 0))
    return pl.pallas_call(
        functools.partial(_router_body, n_experts=n_experts),
        out_shape=[jax.ShapeDtypeStruct((m, LANES), F32), jax.ShapeDtypeStruct((m, LANES), BF16)],
        grid=(m // tm,),
        in_specs=[pl.BlockSpec((tm, half), lambda i: (i, 0)), pl.BlockSpec((2 * half, LANES), lambda i: (0, 0))],
        out_specs=[out, out],
        compiler_params=_params(("parallel",)),
        name="router",
    )(u_packed, w_router_padded)


def _rank_body(sel_ref, rank_ref, cnt_ref, carry_ref):
    @pl.when(pl.program_id(0) == 0)
    def _():
        carry_ref[...] = jnp.zeros_like(carry_ref)

    sel = sel_ref[...]
    n = sel.shape[0]
    t = lax.broadcasted_iota(jnp.int32, (n, n), 0)
    s = lax.broadcasted_iota(jnp.int32, (n, n), 1)
    before = jnp.where(s < t, 1.0, 0.0).astype(BF16)
    carry = carry_ref[...]
    rank_ref[...] = _dot(before, sel) + carry[0:1, :]
    carry = carry + jnp.sum(sel.astype(F32), axis=0, keepdims=True)
    carry_ref[...] = carry
    cnt_ref[...] = carry


def _rank(sel):
    m = sel.shape[0]
    tile = _tile(m, RANK_TILE, ROW_TILE)
    blk = pl.BlockSpec((tile, LANES), lambda i: (i, 0))
    return pl.pallas_call(
        _rank_body,
        out_shape=[jax.ShapeDtypeStruct((m, LANES), F32), jax.ShapeDtypeStruct((8, LANES), F32)],
        grid=(m // tile,),
        in_specs=[blk],
        out_specs=[blk, pl.BlockSpec((8, LANES), lambda i: (0, 0))],
        scratch_shapes=[pltpu.VMEM((8, LANES), F32)],
        compiler_params=_params(("arbitrary",)),
        name="moe_rank",
    )(sel)


def _slots_body(rank_ref, sel_ref, comb_ref, cnt_ref, pos_ref, wts_ref):
    cnt = cnt_ref[...]
    n_tiles = jnp.floor((cnt + (MOE_TILE - 1)) * (1.0 / MOE_TILE))
    a = lax.broadcasted_iota(jnp.int32, (LANES, LANES), 0)
    b = lax.broadcasted_iota(jnp.int32, (LANES, LANES), 1)
    earlier = jnp.where(a < b, 1.0, 0.0).astype(BF16)
    start = _dot(n_tiles.astype(BF16), earlier)[0:1, :] * MOE_TILE
    sel = sel_ref[...].astype(F32) > 0.5
    lane = lax.broadcasted_iota(jnp.int32, sel.shape, 1)
    lane_a = jnp.min(jnp.where(sel, lane, LANES), axis=-1, keepdims=True)
    lane_b = jnp.max(jnp.where(sel, lane, -1), axis=-1, keepdims=True)
    row = start + rank_ref[...]
    comb = comb_ref[...]

    def pick(x, which):
        return jnp.sum(jnp.where(lane == which, x, 0.0), axis=-1, keepdims=True)

    pos_ref[...] = jnp.where(lane == 0, pick(row, lane_a), jnp.where(lane == 1, pick(row, lane_b), 0.0)).astype(jnp.int32)
    wts_ref[...] = jnp.where(lane == 0, pick(comb, lane_a), jnp.where(lane == 1, pick(comb, lane_b), 0.0))


def _slots(rank, sel, comb, counts):
    m = rank.shape[0]
    tm = _tile(m, 1024, ROW_TILE)
    blk = pl.BlockSpec((tm, LANES), lambda i: (i, 0))
    return pl.pallas_call(
        _slots_body,
        out_shape=[jax.ShapeDtypeStruct((m, LANES), jnp.int32), jax.ShapeDtypeStruct((m, LANES), F32)],
        grid=(m // tm,),
        in_specs=[blk, blk, blk, pl.BlockSpec((8, LANES), lambda i: (0, 0))],
        out_specs=[blk, blk],
        compiler_params=_params(("parallel",)),
        name="moe_slots",
    )(rank, sel, comb, counts)


def _row_copy(src_ref, src_row, dst_ref, dst_row, sem):
    return pltpu.make_async_copy(src_ref.at[pl.ds(src_row, 1)], dst_ref.at[pl.ds(dst_row, 1)], sem)


def _dispatch_body(pa_ref, pb_ref, u_ref, xs_in_ref, xs_ref, sems, *, n_tokens):
    del xs_in_ref
    n_chunks = n_tokens // DMA_CHUNK

    def wait_chunk(slot):
        def wait_one(t, carry):
            _row_copy(u_ref, 0, xs_ref, 0, sems.at[slot]).wait()
            return carry
        lax.fori_loop(0, 2 * DMA_CHUNK, wait_one, 0)

    def chunk(c, carry):
        slot = c % 2

        def issue(t, carry2):
            tok = c * DMA_CHUNK + t
            _row_copy(u_ref, tok, xs_ref, pa_ref[tok], sems.at[slot]).start()
            _row_copy(u_ref, tok, xs_ref, pb_ref[tok], sems.at[slot]).start()
            return carry2

        lax.fori_loop(0, DMA_CHUNK, issue, 0)

        @pl.when(c > 0)
        def _():
            wait_chunk(1 - slot)

        return carry

    lax.fori_loop(0, n_chunks, chunk, 0)
    wait_chunk((n_chunks - 1) % 2)


def _dispatch(pos_a, pos_b, u_packed, n_sorted):
    m, half = u_packed.shape
    assert m % DMA_CHUNK == 0
    xs0 = jnp.zeros((n_sorted, half), U32)
    grid_spec = pltpu.PrefetchScalarGridSpec(
        num_scalar_prefetch=2, grid=(1,),
        in_specs=[pl.BlockSpec(memory_space=pl.ANY), pl.BlockSpec(memory_space=pl.ANY)],
        out_specs=pl.BlockSpec(memory_space=pl.ANY),
        scratch_shapes=[pltpu.SemaphoreType.DMA((2,))])
    return pl.pallas_call(
        functools.partial(_dispatch_body, n_tokens=m),
        out_shape=jax.ShapeDtypeStruct((n_sorted, half), U32),
        grid_spec=grid_spec,
        input_output_aliases={3: 0},
        compiler_params=_params(("arbitrary",)),
        name="moe_dispatch",
    )(pos_a, pos_b, u_packed, xs0)


def _expert_glu_body(te_ref, x_ref, wg_ref, wu_ref, o_ref):
    del te_ref
    lo, hi = _unpack_halves(x_ref[...])
    half = lo.shape[1]
    g = _dot(lo, wg_ref[pl.ds(0, half), :]) + _dot(hi, wg_ref[pl.ds(half, half), :])
    u = _dot(lo, wu_ref[pl.ds(0, half), :]) + _dot(hi, wu_ref[pl.ds(half, half), :])
    o_ref[...] = (g * _sigmoid(g) * u).astype(o_ref.dtype)


def _expert_glu(tile_expert, xs, wg, wu, tn):
    n_sorted, half = xs.shape
    _, kdim, n = wg.shape
    w_spec = pl.BlockSpec((None, kdim, tn), lambda j, i, te: (te[i], 0, j))
    grid_spec = pltpu.PrefetchScalarGridSpec(
        num_scalar_prefetch=1, grid=(n // tn, n_sorted // MOE_TILE),
        in_specs=[pl.BlockSpec((MOE_TILE, half), lambda j, i, te: (i, 0)), w_spec, w_spec],
        out_specs=pl.BlockSpec((MOE_TILE, tn), lambda j, i, te: (i, j)))
    return pl.pallas_call(
        _expert_glu_body,
        out_shape=jax.ShapeDtypeStruct((n_sorted, n), BF16),
        grid_spec=grid_spec,
        compiler_params=_params(("parallel", "arbitrary")),
        name="moe_glu",
    )(tile_expert, xs, wg, wu)


def _expert_down_body(te_ref, h_ref, w_ref, o_ref):
    del te_ref
    o_ref[...] = _dot(h_ref[...], w_ref[...])


def _expert_down(tile_expert, hs, wd, tn):
    n_sorted, kdim = hs.shape
    n = wd.shape[2]
    grid_spec = pltpu.PrefetchScalarGridSpec(
        num_scalar_prefetch=1, grid=(n // tn, n_sorted // MOE_TILE),
        in_specs=[pl.BlockSpec((MOE_TILE, kdim), lambda j, i, te: (i, 0)),
                  pl.BlockSpec((None, kdim, tn), lambda j, i, te: (te[i], 0, j))],
        out_specs=pl.BlockSpec((MOE_TILE, tn), lambda j, i, te: (i, j)))
    return pl.pallas_call(
        _expert_down_body,
        out_shape=jax.ShapeDtypeStruct((n_sorted, n), F32),
        grid_spec=grid_spec,
        compiler_params=_params(("parallel", "arbitrary")),
        name="moe_down",
    )(tile_expert, hs, wd)


def _combine_post_body(pa_ref, pb_ref, os_ref, h_ref, wts_ref, gp_ref, gate_ref, ho_ref, buf, sems):
    i = pl.program_id(0)
    n = pl.num_programs(0)
    tile = h_ref.shape[0]
    slot = i % 2

    def issue(tile_idx, dst_slot):
        def one(t, carry):
            tok = tile_idx * tile + t
            _row_copy(os_ref, pa_ref[tok], buf.at[dst_slot, 0], t, sems.at[dst_slot]).start()
            _row_copy(os_ref, pb_ref[tok], buf.at[dst_slot, 1], t, sems.at[dst_slot]).start()
            return carry
        lax.fori_loop(0, tile, one, 0)

    @pl.when(i == 0)
    def _():
        issue(0, 0)

    @pl.when(i + 1 < n)
    def _():
        issue(i + 1, 1 - slot)

    def wait_one(t, carry):
        _row_copy(os_ref, 0, buf.at[slot, 0], 0, sems.at[slot]).wait()
        return carry

    lax.fori_loop(0, 2 * tile, wait_one, 0)
    wts = wts_ref[...]
    y = wts[:, 0:1] * buf[slot, 0] + wts[:, 1:2] * buf[slot, 1]
    ho_ref[...] = h_ref[...] + gate_ref[...] * _rms(y, gp_ref[...])


def _combine_post(pos_a, pos_b, os, h, wts, g_post, gate, group, n_rows):
    d = h.shape[1]
    row = lambda i, pa, pb: (i, 0)
    grid_spec = pltpu.PrefetchScalarGridSpec(
        num_scalar_prefetch=2, grid=(n_rows // ROW_TILE,),
        in_specs=[pl.BlockSpec(memory_space=pl.ANY),
                  pl.BlockSpec((ROW_TILE, d), row),
                  pl.BlockSpec((ROW_TILE, LANES), row),
                  pl.BlockSpec((1, d), lambda i, pa, pb: (0, 0)),
                  pl.BlockSpec((None, 1, d), lambda i, pa, pb: (group(i), 0, 0))],
        out_specs=pl.BlockSpec((ROW_TILE, d), row),
        scratch_shapes=[pltpu.VMEM((2, 2, ROW_TILE, d), F32), pltpu.SemaphoreType.DMA((2,))])
    return pl.pallas_call(
        _combine_post_body,
        out_shape=jax.ShapeDtypeStruct((n_rows, d), F32),
        grid_spec=grid_spec,
        compiler_params=_params(("arbitrary",)),
        name="moe_combine_post",
    )(pos_a, pos_b, os, h, wts, g_post, gate)


def _split3(x):
    hi = x.astype(BF16)
    r1 = x - hi.astype(F32)
    mid = r1.astype(BF16)
    lo = (r1 - mid.astype(F32)).astype(BF16)
    return hi, mid, lo


def _gates_body(g_ref, b_ref, o_ref, *, n_heads):
    pre = g_ref[...] + b_ref[...]
    log_f = jnp.minimum(pre, 0.0) - jnp.log(1.0 + jnp.exp(-jnp.abs(pre)))
    lc = pre.shape[0]
    t = lax.broadcasted_iota(jnp.int32, (lc, lc), 0)
    s = lax.broadcasted_iota(jnp.int32, (lc, lc), 1)
    tri_f = jnp.where(s <= t, 1.0, 0.0).astype(BF16)
    tri_b = jnp.where(s >= t, 1.0, 0.0).astype(BF16)
    hi, mid, lo = _split3(log_f)
    cs_f = _dot(tri_f, hi) + _dot(tri_f, mid) + _dot(tri_f, lo)
    cs_b = _dot(tri_b, hi) + _dot(tri_b, mid) + _dot(tri_b, lo)
    lane = lax.broadcasted_iota(jnp.int32, pre.shape, 1)
    is_ff = (lane >= n_heads) & (lane < 2 * n_heads)
    is_fb = (lane >= 3 * n_heads) & (lane < 4 * n_heads)
    o_ref[...] = jnp.where(is_ff, cs_f, jnp.where(is_fb, cs_b, pre))


def _gates(g_raw, bias_row, n_heads):
    t_all = g_raw.shape[0]
    blk = pl.BlockSpec((MLSTM_CHUNK, LANES), lambda i: (i, 0))
    return pl.pallas_call(
        functools.partial(_gates_body, n_heads=n_heads),
        out_shape=jax.ShapeDtypeStruct((t_all, LANES), F32),
        grid=(t_all // MLSTM_CHUNK,),
        in_specs=[blk, pl.BlockSpec((1, LANES), lambda i: (0, 0))],
        out_specs=blk,
        compiler_params=_params(("parallel",)),
        name="mlstm_gates",
    )(g_raw, bias_row)


def _mlstm_chunk(q, k, v, gc, gr, state, d, rev):
    c_st, n_st, m_st = state
    lc = q.shape[0]
    li_c, bc_c = gc[:, 2 * d:2 * d + 1], gc[:, 2 * d + 1:2 * d + 2]
    li_r, bc_r = gr[2 * d:2 * d + 1, :], gr[2 * d + 1:2 * d + 2, :]
    b_last = bc_r[:, 0:1] if rev else bc_r[:, lc - 1:lc]
    g_c = b_last - bc_c + li_c
    g_r = b_last - bc_r + li_r
    m_new = jnp.maximum(b_last + m_st, jnp.max(g_r, axis=-1, keepdims=True))
    decay = jnp.exp(b_last + m_st - m_new)
    kw = k.astype(F32) * jnp.exp(g_c - m_new)
    c_new = decay * c_st + _dot_tn(kw.astype(BF16), v)
    n_new = decay * n_st + jnp.sum(kw, axis=0, keepdims=True)

    t = lax.broadcasted_iota(jnp.int32, (lc, lc), 0)
    j = lax.broadcasted_iota(jnp.int32, (lc, lc), 1)
    seen = (j >= t) if rev else (j <= t)
    dmat = jnp.where(seen, bc_c - bc_r + li_r, NEG)
    m_inter = bc_c + m_st
    m_t = jnp.maximum(m_inter, jnp.max(dmat, axis=-1, keepdims=True))
    s = _dot_nt(q, k) * jnp.exp(dmat - m_t)
    inter = jnp.exp(m_inter - m_t)
    num = _dot(s.astype(BF16), v) + inter * _dot(q, c_st.astype(BF16))
    den = jnp.sum(s, axis=-1, keepdims=True) + inter * jnp.sum(q.astype(F32) * n_st, axis=-1, keepdims=True)
    h = num / jnp.maximum(jnp.abs(den), jnp.exp(-m_t))
    return h, (c_new, n_new, m_new)


def _mlstm_body(qc_ref, kc_ref, vc_ref, oc_ref, ql_ref, kl_ref, vl_ref, ol_ref,
                gcc_ref, grc_ref, gcl_ref, grl_ref, cos_ref, sin_ref, ng_ref, *rest, with_ctx_out):
    if with_ctx_out:
        yl_ref, yc_ref, qs_ref, ks_ref, hf_ref, hb_ref = rest
    else:
        yl_ref, qs_ref, ks_ref, hf_ref, hb_ref = rest
    lc = MLSTM_CHUNK
    n_lat = ql_ref.shape[0] // lc
    n_ctx = qc_ref.shape[0] // lc
    scale = HEAD_DIM ** -0.5
    lane = lax.broadcasted_iota(jnp.int32, (lc, HEAD_DIM), 1)
    first_quarter = (lane % (HEAD_DIM // 2)) < (HEAD_DIM // 4)

    def rope(x, rows):
        partner = jnp.where(first_quarter, pltpu.roll(x, HEAD_DIM - HEAD_DIM // 4, 1), pltpu.roll(x, HEAD_DIM // 4, 1))
        return x * cos_ref[rows, :] + partner * sin_ref[rows, :]

    def rope_chunk(i, carry):
        rows = pl.ds(pl.multiple_of(i * lc, lc), lc)
        qs_ref[rows, :] = rope(ql_ref[rows, :].astype(F32) * scale, rows).astype(BF16)
        ks_ref[rows, :] = rope(kl_ref[rows, :].astype(F32), rows).astype(BF16)
        return carry

    lax.fori_loop(0, n_lat, rope_chunk, 0)

    ng = ng_ref[...]

    def finish(h, o_gate):
        return (_sigmoid(o_gate.astype(F32)) * _rms(h, ng)).astype(BF16)

    zero = (jnp.zeros((HEAD_DIM, HEAD_DIM), F32), jnp.zeros((1, HEAD_DIM), F32), jnp.zeros((1, 1), F32))

    st = [zero, zero]
    for i in range(n_ctx):
        for d in range(2):
            ci = i if d == 0 else n_ctx - 1 - i
            rows = pl.ds(ci * lc, lc)
            q = (qc_ref[rows, :].astype(F32) * scale).astype(BF16)
            h, st[d] = _mlstm_chunk(q, kc_ref[rows, :], vc_ref[rows, :], gcc_ref[rows, :], grc_ref[:, rows],
                                    st[d], d, d == 1)
            if with_ctx_out:
                (hf_ref if d == 0 else hb_ref)[rows, :] = h
    if with_ctx_out:
        for i in range(n_ctx):
            rows = pl.ds(i * lc, lc)
            yc_ref[rows, :] = finish(hf_ref[rows, :] + hb_ref[rows, :], oc_ref[rows, :])

    def lat_step(i, carry):
        st_f, st_b = carry
        out = []
        for d, st_d in ((0, st_f), (1, st_b)):
            ci = i if d == 0 else n_lat - 1 - i
            rows = pl.ds(pl.multiple_of(ci * lc, lc), lc)
            h, st_d = _mlstm_chunk(qs_ref[rows, :], ks_ref[rows, :], vl_ref[rows, :], gcl_ref[rows, :],
                                   grl_ref[:, rows], st_d, d, d == 1)
            (hf_ref if d == 0 else hb_ref)[rows, :] = h
            out.append(st_d)
        return tuple(out)

    lax.fori_loop(0, n_lat, lat_step, (st[0], st[1]))

    def finish_chunk(i, carry):
        rows = pl.ds(pl.multiple_of(i * lc, lc), lc)
        yl_ref[rows, :] = finish(hf_ref[rows, :] + hb_ref[rows, :], ol_ref[rows, :])
        return carry

    lax.fori_loop(0, n_lat, finish_chunk, 0)


def _mlstm(p, g_col, g_row, cos, sin_signed, norm_g, n_batch, seq, ctx_len, n_heads, with_ctx_out):
    ctx_blk0 = n_batch * seq // ctx_len

    def col(kind, lat):
        rows = seq if lat else ctx_len
        if lat:
            return pl.BlockSpec((rows, HEAD_DIM), lambda b, h: (b, kind * n_heads + h))
        return pl.BlockSpec((rows, HEAD_DIM), lambda b, h: (ctx_blk0 + b, kind * n_heads + h))

    in_specs = ([col(kind, False) for kind in range(4)] + [col(kind, True) for kind in range(4)] + [
        pl.BlockSpec((None, ctx_len, 4), lambda b, h: (h, ctx_blk0 + b, 0)),
        pl.BlockSpec((None, 4, ctx_len), lambda b, h: (h, 0, ctx_blk0 + b)),
        pl.BlockSpec((None, seq, 4), lambda b, h: (h, b, 0)),
        pl.BlockSpec((None, 4, seq), lambda b, h: (h, 0, b)),
        pl.BlockSpec((seq, HEAD_DIM), lambda b, h: (0, 0)),
        pl.BlockSpec((seq, HEAD_DIM), lambda b, h: (0, 0)),
        pl.BlockSpec((None, 1, HEAD_DIM), lambda b, h: (h, 0, 0)),
    ])
    out_shape = [jax.ShapeDtypeStruct((n_batch * seq, n_heads * HEAD_DIM), BF16)]
    out_specs = [pl.BlockSpec((seq, HEAD_DIM), lambda b, h: (b, h))]
    if with_ctx_out:
        out_shape.append(jax.ShapeDtypeStruct((n_batch * ctx_len, n_heads * HEAD_DIM), BF16))
        out_specs.append(pl.BlockSpec((ctx_len, HEAD_DIM), lambda b, h: (b, h)))
    return pl.pallas_call(
        functools.partial(_mlstm_body, with_ctx_out=with_ctx_out),
        out_shape=out_shape,
        grid=(n_batch, n_heads),
        in_specs=in_specs,
        out_specs=out_specs,
        scratch_shapes=[pltpu.VMEM((seq, HEAD_DIM), BF16), pltpu.VMEM((seq, HEAD_DIM), BF16),
                        pltpu.VMEM((seq, HEAD_DIM), F32), pltpu.VMEM((seq, HEAD_DIM), F32)],
        compiler_params=_params(("parallel", "parallel")),
        name="mlstm",
    )(p, p, p, p, p, p, p, p, g_col, g_row, g_col, g_row, cos, sin_signed, norm_g)


def _conv_body(ap_ref, gp_ref, a_ref, g_ref, an_ref, gn_ref, w_ref, b_ref, ng_ref, o_ref, u_ref, c_ref,
               *, n_lat_tiles, lat_tiles_per_seq, width):
    i = pl.program_id(0)
    is_lat = i < n_lat_tiles
    pos = i % lat_tiles_per_seq
    first = jnp.logical_or(jnp.logical_not(is_lat), pos == 0)
    last = jnp.logical_or(jnp.logical_not(is_lat), pos == lat_tiles_per_seq - 1)

    def glu(x_ref, y_ref):
        return x_ref[...].astype(F32) * _sigmoid(y_ref[...].astype(F32))

    tile = a_ref.shape[0]
    u_ref[pl.ds(0, HALO), :] = jnp.where(first, 0.0, 1.0) * glu(ap_ref, gp_ref)
    u_ref[pl.ds(HALO, tile), :] = glu(a_ref, g_ref)
    u_ref[pl.ds(HALO + tile, HALO), :] = jnp.where(last, 0.0, 1.0) * glu(an_ref, gn_ref)

    strip, cblk = 32, 256
    n_ch = a_ref.shape[1]
    cblk = min(cblk, n_ch)
    base = HALO - width // 2

    for r0 in range(0, tile, strip):
        for cb in range(n_ch // cblk):
            cols = pl.ds(cb * cblk, cblk)
            acc = jnp.zeros((strip, cblk), F32)
            for k in range(width):
                acc += w_ref[pl.ds(k, 1), cols] * u_ref[pl.ds(r0 + base + k, strip), cols]
            c_ref[pl.ds(r0, strip), cols] = acc
    y = _rms(c_ref[...] + b_ref[...], ng_ref[...])
    o_ref[...] = (y * _sigmoid(y)).astype(o_ref.dtype)


def _conv(p, conv_w, conv_b, conv_norm, a_col_blk, n_rows, n_lat_tiles, lat_tiles_per_seq):
    width, n_ch = conv_w.shape
    assert width // 2 <= HALO
    per = ROW_TILE // HALO
    last_halo = p.shape[0] // HALO - 1

    def cur(c):
        return pl.BlockSpec((ROW_TILE, n_ch), lambda i: (i, c))

    def prev(c):
        return pl.BlockSpec((HALO, n_ch), lambda i: (jnp.maximum(i * per - 1, 0), c))

    def nxt(c):
        return pl.BlockSpec((HALO, n_ch), lambda i: (jnp.minimum((i + 1) * per, last_halo), c))

    vec = pl.BlockSpec((1, n_ch), lambda i: (0, 0))
    a, g = a_col_blk, a_col_blk + 1
    return pl.pallas_call(
        functools.partial(_conv_body, n_lat_tiles=n_lat_tiles, lat_tiles_per_seq=lat_tiles_per_seq, width=width),
        out_shape=jax.ShapeDtypeStruct((n_rows, n_ch), BF16),
        grid=(n_rows // ROW_TILE,),
        in_specs=[prev(a), prev(g), cur(a), cur(g), nxt(a), nxt(g),
                  pl.BlockSpec((width, n_ch), lambda i: (0, 0)), vec, vec],
        out_specs=pl.BlockSpec((ROW_TILE, n_ch), lambda i: (i, 0)),
        scratch_shapes=[pltpu.VMEM((ROW_TILE + 2 * HALO, n_ch), F32), pltpu.VMEM((ROW_TILE, n_ch), F32)],
        compiler_params=_params(("parallel",)),
        name="conformer_conv",
    )(p, p, p, p, p, p, conv_w, conv_b, conv_norm)


def _na_body(*refs, rows, with_ctx_out):
    if with_ctx_out:
        q_ref, k_ref, v_ref, kc_ref, vc_ref, bias_ref, qc_ref, o_ref, oc_ref = refs
    else:
        q_ref, k_ref, v_ref, kc_ref, vc_ref, bias_ref, o_ref = refs
    scale = HEAD_DIM ** -0.5
    wr = NA_ROWS
    kc = kc_ref[...]
    vc = vc_ref[...]

    def row_block(r, carry):
        row_start = jnp.clip(r - wr // 2, 0, rows - wr)
        shift = row_start - r + NA_ROWS // 2
        q_r = q_ref[pl.ds(pl.multiple_of(r * GRID_W, GRID_W), GRID_W), :]
        win = pl.ds(pl.multiple_of(row_start * GRID_W, GRID_W), wr * GRID_W)
        s_win = _dot_nt(q_r, k_ref[win, :]) * scale + bias_ref[shift + NA_ROWS // 2 - 1]
        s_ctx = _dot_nt(q_r, kc) * scale
        m = jnp.maximum(jnp.max(s_win, axis=-1, keepdims=True), jnp.max(s_ctx, axis=-1, keepdims=True))
        p_win = jnp.exp(s_win - m)
        p_ctx = jnp.exp(s_ctx - m)
        denom = jnp.sum(p_win, axis=-1, keepdims=True) + jnp.sum(p_ctx, axis=-1, keepdims=True)
        out = (_dot(p_win.astype(BF16), v_ref[win, :]) + _dot(p_ctx.astype(BF16), vc)) / denom
        o_ref[pl.ds(pl.multiple_of(r * GRID_W, GRID_W), GRID_W), :] = out.astype(o_ref.dtype)
        return carry

    lax.fori_loop(0, rows, row_block, 0)

    if with_ctx_out:
        s = _dot_nt(qc_ref[...], kc) * scale
        p = jnp.exp(s - jnp.max(s, axis=-1, keepdims=True))
        oc_ref[...] = (_dot(p.astype(BF16), vc) / jnp.sum(p, axis=-1, keepdims=True)).astype(oc_ref.dtype)


def _na(p, bias, col0, n_batch, seq, ctx_len, n_heads, with_ctx_out):
    ctx_blk0 = n_batch * seq // ctx_len
    rows = seq // GRID_W

    def lat(kind):
        return pl.BlockSpec((seq, HEAD_DIM), lambda b, h: (b, col0 + kind * n_heads + h))

    def ctx(kind):
        return pl.BlockSpec((ctx_len, HEAD_DIM), lambda b, h: (ctx_blk0 + b, col0 + kind * n_heads + h))

    in_specs = [lat(0), lat(1), lat(2), ctx(1), ctx(2),
                pl.BlockSpec((None, NA_ROWS, GRID_W, NA_ROWS * GRID_W), lambda b, h: (h, 0, 0, 0))]
    args = [p, p, p, p, p, bias]
    out_shape = [jax.ShapeDtypeStruct((n_batch * seq, n_heads * HEAD_DIM), BF16)]
    out_specs = [pl.BlockSpec((seq, HEAD_DIM), lambda b, h: (b, h))]
    if with_ctx_out:
        in_specs.append(ctx(0))
        args.append(p)
        out_shape.append(jax.ShapeDtypeStruct((n_batch * ctx_len, n_heads * HEAD_DIM), BF16))
        out_specs.append(pl.BlockSpec((ctx_len, HEAD_DIM), lambda b, h: (b, h)))
    return pl.pallas_call(
        functools.partial(_na_body, rows=rows, with_ctx_out=with_ctx_out),
        out_shape=out_shape,
        grid=(n_batch, n_heads),
        in_specs=in_specs,
        out_specs=out_specs,
        compiler_params=_params(("parallel", "parallel")),
        name="neighbourhood_attention",
    )(*args)


def _na_bias_body(r_ref, o_ref):
    n_taps = 2 * NA_COLS - 1
    width = r_ref.shape[1]
    q = lax.broadcasted_iota(jnp.int32, (GRID_W, width), 0)
    k = lax.broadcasted_iota(jnp.int32, (GRID_W, width), 1) % GRID_W
    tap = k - q + (NA_COLS - 1)
    wide = jnp.zeros((GRID_W, width), F32)
    for j in range(n_taps):
        wide = jnp.where(tap == j, r_ref[pl.ds(j, 1), :], wide)
    col_start = jnp.clip(q - NA_COLS // 2, 0, GRID_W - NA_COLS)
    wide = jnp.where(k >= col_start, jnp.where(k < col_start + NA_COLS, wide, NEG), NEG)
    win = o_ref.shape[2]
    for s in range(NA_ROWS):
        o_ref[s] = wide[:, s * GRID_W:s * GRID_W + win]


def _na_bias_table(rpb, rows):
    wr = min(NA_ROWS, rows)
    n_heads, n_dr, n_taps = rpb.shape
    assert n_dr == NA_ROWS + wr - 1 and n_taps == 2 * NA_COLS - 1
    width = -(-(n_dr * GRID_W) // LANES) * LANES
    spread = jnp.repeat(rpb.transpose(0, 2, 1), GRID_W, axis=2)
    spread = jnp.pad(spread, ((0, 0), (0, 32 - n_taps), (0, width - n_dr * GRID_W)))
    return pl.pallas_call(
        _na_bias_body,
        out_shape=jax.ShapeDtypeStruct((n_heads, NA_ROWS, GRID_W, wr * GRID_W), F32),
        grid=(n_heads,),
        in_specs=[pl.BlockSpec((None, 32, width), lambda h: (h, 0, 0))],
        out_specs=pl.BlockSpec((None, NA_ROWS, GRID_W, wr * GRID_W), lambda h: (h, 0, 0, 0)),
        compiler_params=_params(("parallel",)),
        name="na_bias",
    )(spread)


def _rope_tables(length):
    t = jnp.arange(length)
    n_freq = HEAD_DIM // 4
    freqs = ROPE_BASE ** (-jnp.arange(n_freq, dtype=F32) / n_freq)
    ang_r = (t // GRID_W).astype(F32)[:, None] * freqs
    ang_c = (t % GRID_W).astype(F32)[:, None] * freqs
    ang = jnp.concatenate([ang_r, ang_r, ang_c, ang_c], axis=-1)
    sign = jnp.concatenate([-jnp.ones(n_freq), jnp.ones(n_freq), -jnp.ones(n_freq), jnp.ones(n_freq)])
    return jnp.cos(ang), jnp.sin(ang) * sign


def kernel(x, c, ctx, c_ctx, w_mod, b_mod, g_norm, w_in, w_out, mlstm_gate_bias, mlstm_norm, conv_w, conv_b,
           conv_norm, na_rpb, ffn_w_gate, ffn_w_up, ffn_w_down, router_w, moe_w_gate, moe_w_up, moe_w_down):
    n_batch, seq, d = x.shape
    ctx_len = ctx.shape[1]
    depth = w_mod.shape[0]
    m_width = mlstm_norm.shape[1]
    c_width = conv_w.shape[2]
    n_width = d - m_width - c_width
    mh, nh = m_width // HEAD_DIM, n_width // HEAD_DIM
    n_experts = router_w.shape[2]
    t_lat, t_ctx = n_batch * seq, n_batch * ctx_len
    t_all = t_lat + t_ctx
    assert seq % MLSTM_CHUNK == 0 and ctx_len % MLSTM_CHUNK == 0 and ctx_len == ROW_TILE
    assert t_lat % ctx_len == 0 and (4 * m_width) % c_width == 0 and 4 * mh <= LANES
    n_lat_tiles, lat_tiles_per_seq = t_lat // ROW_TILE, seq // ROW_TILE
    group = _group_map(n_lat_tiles, lat_tiles_per_seq, n_batch)

    h = jnp.concatenate([x.reshape(t_lat, d), ctx.reshape(t_ctx, d)], axis=0)
    cond = jnp.concatenate([c, jnp.broadcast_to(c_ctx[None], (8 - n_batch, d))], axis=0)
    mod = _modulation(cond, w_mod, b_mod)
    grp_rows = jnp.concatenate([jnp.arange(n_batch), jnp.full((n_batch,), n_batch)])
    cos, sin_signed = _rope_tables(seq)
    tm_all, tm_lat = _tile(t_all, 1024, ROW_TILE), _tile(t_lat, 1024, ROW_TILE)

    def mods(layer):
        m6 = mod[layer][grp_rows].reshape(2 * n_batch, N_MOD, 1, d)
        return [m6[:, k] for k in range(N_MOD)]

    gv = lambda layer, k: g_norm[layer, k].reshape(1, d)

    sh1, sc1, _, _, _, _ = mods(0)
    u = _normmod(h, gv(0, 0), sh1, sc1, group, t_all)

    for layer in range(depth):
        last = layer == depth - 1
        with_ctx_out = not last
        sh1, sc1, gt1, sh2, sc2, gt2 = mods(layer)
        rows_out = t_lat if last else t_all
        tm_out = tm_lat if last else tm_all

        w = w_in[layer]
        cut_g, cut_c, cut_n = 4 * m_width, 4 * m_width + 4 * mh, 4 * m_width + 4 * mh + 2 * c_width
        w_main = jnp.concatenate([w[:, :cut_g], w[:, cut_c:]], axis=1).astype(BF16)
        w_gate = jnp.pad(w[:, cut_g:cut_c], ((0, 0), (0, LANES - 4 * mh))).astype(BF16)
        n_main = w_main.shape[1]
        p = _matmul([u], [w_main], BF16, t_all, tm_all, _tile(n_main, 1280, 2 * LANES), name="w_in")
        g_raw = _matmul([u], [w_gate], F32, t_all, tm_all, LANES, name="w_in_gates")

        bias_row = jnp.pad(mlstm_gate_bias[layer].reshape(1, 4 * mh), ((0, 0), (0, LANES - 4 * mh)))
        g_tab = _gates(g_raw, bias_row, mh)[:, :4 * mh].reshape(t_all, 4, mh)
        g_col, g_row = g_tab.transpose(2, 0, 1), g_tab.transpose(2, 1, 0)
        ym = _mlstm(p, g_col, g_row, cos, sin_signed, mlstm_norm[layer].reshape(mh, 1, HEAD_DIM),
                    n_batch, seq, ctx_len, mh, with_ctx_out)
        yc = _conv(p, conv_w[layer], conv_b[layer].reshape(1, c_width), conv_norm[layer].reshape(1, c_width),
                   cut_g // c_width, rows_out, n_lat_tiles, lat_tiles_per_seq)
        bias = _na_bias_table(na_rpb[layer], seq // GRID_W)
        yn = _na(p, bias, (cut_g + 2 * c_width) // HEAD_DIM, n_batch, seq, ctx_len, nh, with_ctx_out)
        if with_ctx_out:
            ym, yn = jnp.concatenate(ym, axis=0), jnp.concatenate(yn, axis=0)
        else:
            ym, yn = ym[0], yn[0]

        wo = w_out[layer]
        wo_parts = [wo[:m_width].astype(BF16), wo[m_width:m_width + c_width].astype(BF16),
                    wo[m_width + c_width:].astype(BF16)]
        mix = _matmul([ym, yc, yn], wo_parts, F32, rows_out, tm_out, _tile(d, 1024, 2 * LANES), name="w_out")
        moe = layer % 2 == 1
        h, u2 = _post(h, mix, gv(layer, 1), gt1, group, rows_out, nxt=(gv(layer, 2), sh2, sc2), pack_next=moe)

        j = layer // 2
        if not moe:
            d_ff = ffn_w_gate.shape[2]
            hid = _glu(u2, ffn_w_gate[j][None].astype(BF16), ffn_w_up[j][None].astype(BF16),
                       rows_out, tm_out, _tile(d_ff, 512, 2 * LANES))
            f = _matmul([hid], [ffn_w_down[j].astype(BF16)], F32, rows_out, tm_out, _tile(d, 1024, 2 * LANES),
                        tk=_tile(d_ff, 2048, 2 * LANES), name="ffn_down")
        else:
            d_e = moe_w_gate.shape[3]
            w_r = jnp.pad(router_w[j], ((0, 0), (0, LANES - n_experts))).astype(BF16)
            comb, sel = _router(u2, w_r, rows_out, n_experts)
            rank, counts = _rank(sel)
            pos, wts = _slots(rank, sel, comb, counts)
            pos_a, pos_b = pos[:, 0], pos[:, 1]
            n_sorted = TOP_K * rows_out + n_experts * MOE_TILE
            tiles_per_expert = (counts[0, :n_experts].astype(jnp.int32) + MOE_TILE - 1) // MOE_TILE
            tile_end = jnp.cumsum(tiles_per_expert)
            tile_expert = jnp.minimum(
                jnp.sum(jnp.arange(n_sorted // MOE_TILE)[:, None] >= tile_end[None, :], axis=1), n_experts - 1
            ).astype(jnp.int32)
            xs = _dispatch(pos_a, pos_b, u2, n_sorted)
            hs = _expert_glu(tile_expert, xs, moe_w_gate[j].astype(BF16), moe_w_up[j].astype(BF16),
                             _tile(d_e, 1024, 2 * LANES))
            os = _expert_down(tile_expert, hs, moe_w_down[j].astype(BF16), _tile(d, 1024, 2 * LANES))

        if moe:
            assert last
            h = _combine_post(pos_a, pos_b, os, h, wts, gv(layer, 3), gt2, group, rows_out)
        elif last:
            (h,) = _post(h, f, gv(layer, 3), gt2, group, rows_out)
        else:
            nsh1, nsc1 = mods(layer + 1)[:2]
            h, u = _post(h, f, gv(layer, 3), gt2, group, rows_out, nxt=(gv(layer + 1, 0), nsh1, nsc1))

    return h[:t_lat].reshape(n_batch, seq, d)
```

```python
import functools

import jax
import jax.numpy as jnp
from jax import lax
from jax.experimental import pallas as pl
from jax.experimental.pallas import tpu as pltpu

HEAD_DIM = 128
GRID_W = 64
NA_ROWS = 8
NA_COLS = 16
NA_GROUP = 4
NA_UNION = NA_ROWS + NA_GROUP - 1
NA_PAD_TILES = NA_UNION - NA_ROWS
ROPE_BASE = 10000.0
N_MOD = 6
TOP_K = 2
EPS = 1e-6
MLSTM_CHUNK = 256
ROW_TILE = 256
HALO = 16
LANES = 128
NEG = -1e30
VMEM_LIMIT = 56 * 1024 * 1024
MOE_TILE = 256
RANK_TILE = 512

F32 = jnp.float32
BF16 = jnp.bfloat16
U32 = jnp.uint32


def _params(sem):
    return pltpu.CompilerParams(dimension_semantics=sem, vmem_limit_bytes=VMEM_LIMIT)


def _tile(n, target, mult=LANES):
    if n <= target:
        return n
    for step in (mult, LANES, 8):
        for t in range((target // step) * step, 0, -step):
            if n % t == 0:
                return t
    raise ValueError((n, target, mult))


def _dot(a, b):
    return jnp.dot(a, b, preferred_element_type=F32)


def _dot_nt(a, b):
    return lax.dot_general(a, b, (((1,), (1,)), ((), ())), preferred_element_type=F32)


def _sigmoid(x):
    return 1.0 / (1.0 + jnp.exp(-x))


def _rms(x, g):
    return x * lax.rsqrt(jnp.mean(x * x, axis=-1, keepdims=True) + EPS) * g


def _mod_body(c_ref, w_ref, b_ref, o_ref):
    k = pl.program_id(2)
    c = c_ref[...]
    s = (c * _sigmoid(c)).astype(BF16)
    part = _dot(s, w_ref[...].astype(BF16))

    @pl.when(k == 0)
    def _():
        o_ref[...] = part + b_ref[...]

    @pl.when(k > 0)
    def _():
        o_ref[...] += part


def _modulation(cond, w_mod, b_mod):
    depth, d, n = w_mod.shape
    tn, tk = _tile(n, 2048), _tile(d, 1024)
    return pl.pallas_call(
        _mod_body,
        out_shape=jax.ShapeDtypeStruct((depth, 8, n), F32),
        grid=(depth, n // tn, d // tk),
        in_specs=[pl.BlockSpec((8, tk), lambda l, j, k: (0, k)),
                  pl.BlockSpec((None, tk, tn), lambda l, j, k: (l, k, j)),
                  pl.BlockSpec((None, 1, tn), lambda l, j, k: (l, 0, j))],
        out_specs=pl.BlockSpec((None, 8, tn), lambda l, j, k: (l, 0, j)),
        compiler_params=_params(("parallel", "parallel", "arbitrary")),
        name="modulation",
    )(cond, w_mod, b_mod.reshape(depth, 1, n))


def _group_map(n_lat_tiles, lat_tiles_per_seq, n_batch):
    def group(i):
        return jnp.where(i < n_lat_tiles, i // lat_tiles_per_seq, n_batch + (i - n_lat_tiles))
    return group


def _normmod_body(h_ref, g_ref, sh_ref, sc_ref, u_ref):
    y = _rms(h_ref[...], g_ref[...])
    u_ref[...] = (y * (1.0 + sc_ref[...]) + sh_ref[...]).astype(u_ref.dtype)


def _normmod(h, g, sh, sc, group, n_rows):
    d = h.shape[1]
    row = pl.BlockSpec((ROW_TILE, d), lambda i: (i, 0))
    vec = pl.BlockSpec((1, d), lambda i: (0, 0))
    per_group = pl.BlockSpec((None, 1, d), lambda i: (group(i), 0, 0))
    return pl.pallas_call(
        _normmod_body,
        out_shape=jax.ShapeDtypeStruct((n_rows, d), BF16),
        grid=(n_rows // ROW_TILE,),
        in_specs=[row, vec, per_group, per_group],
        out_specs=row,
        compiler_params=_params(("parallel",)),
        name="normmod",
    )(h, g, sh, sc)


def _pack_halves(x):
    half = x.shape[1] // 2
    bits = lax.bitcast_convert_type(x.astype(BF16).astype(F32), U32)
    return (bits[:, half:] & jnp.uint32(0xFFFF0000)) | lax.shift_right_logical(bits[:, :half], jnp.uint32(16))


def _unpack_halves(w):
    lo = lax.bitcast_convert_type(lax.shift_left(w, jnp.uint32(16)), F32).astype(BF16)
    hi = lax.bitcast_convert_type(w & jnp.uint32(0xFFFF0000), F32).astype(BF16)
    return lo, hi


def _post_body(h_ref, y_ref, gp_ref, gate_ref, *rest, with_next, pack_next):
    h = h_ref[...] + gate_ref[...] * _rms(y_ref[...], gp_ref[...])
    if with_next:
        gn_ref, sh_ref, sc_ref, ho_ref, u_ref = rest
        ho_ref[...] = h
        u = _rms(h, gn_ref[...]) * (1.0 + sc_ref[...]) + sh_ref[...]
        u_ref[...] = _pack_halves(u) if pack_next else u.astype(u_ref.dtype)
    else:
        (ho_ref,) = rest
        ho_ref[...] = h


def _post(h, y, g_post, gate, group, n_rows, nxt=None, pack_next=False):
    d = h.shape[1]
    row = pl.BlockSpec((ROW_TILE, d), lambda i: (i, 0))
    vec = pl.BlockSpec((1, d), lambda i: (0, 0))
    per_group = pl.BlockSpec((None, 1, d), lambda i: (group(i), 0, 0))
    in_specs = [row, row, vec, per_group]
    args = [h, y, g_post, gate]
    out_shape = [jax.ShapeDtypeStruct((n_rows, d), F32)]
    out_specs = [row]
    if nxt is not None:
        in_specs += [vec, per_group, per_group]
        args += list(nxt)
        if pack_next:
            out_shape.append(jax.ShapeDtypeStruct((n_rows, d // 2), U32))
            out_specs.append(pl.BlockSpec((ROW_TILE, d // 2), lambda i: (i, 0)))
        else:
            out_shape.append(jax.ShapeDtypeStruct((n_rows, d), BF16))
            out_specs.append(row)
    return pl.pallas_call(
        functools.partial(_post_body, with_next=nxt is not None, pack_next=pack_next),
        out_shape=out_shape,
        grid=(n_rows // ROW_TILE,),
        in_specs=in_specs,
        out_specs=out_specs,
        compiler_params=_params(("parallel",)),
        name="post",
    )(*args)


def _mm_body(*refs, n_a, nk):
    a_refs, b_refs, o_ref = refs[:n_a], refs[n_a:2 * n_a], refs[2 * n_a]
    part = _dot(a_refs[0][...], b_refs[0][...])
    for a_ref, b_ref in zip(a_refs[1:], b_refs[1:]):
        part += _dot(a_ref[...], b_ref[...])
    if nk == 1:
        o_ref[...] = part.astype(o_ref.dtype)
        return
    acc_ref = refs[2 * n_a + 1]
    k = pl.program_id(2)

    @pl.when(k == 0)
    def _():
        acc_ref[...] = part

    @pl.when(k > 0)
    def _():
        acc_ref[...] += part

    @pl.when(k == nk - 1)
    def _():
        o_ref[...] = acc_ref[...].astype(o_ref.dtype)


def _matmul(a_list, b_list, out_dtype, m, tm, tn, tk=None, name="matmul"):
    n = b_list[0].shape[1]
    n_a = len(a_list)
    if tk is None:
        nk = 1
        in_specs = ([pl.BlockSpec((tm, a.shape[1]), lambda i, j: (i, 0)) for a in a_list]
                    + [pl.BlockSpec((b.shape[0], tn), lambda i, j: (0, j)) for b in b_list])
        grid = (m // tm, n // tn)
        out_spec = pl.BlockSpec((tm, tn), lambda i, j: (i, j))
        scratch = []
        sem = ("parallel", "parallel")
    else:
        assert n_a == 1
        kdim = a_list[0].shape[1]
        nk = kdim // tk
        in_specs = [pl.BlockSpec((tm, tk), lambda i, j, k: (i, k)),
                    pl.BlockSpec((tk, tn), lambda i, j, k: (k, j))]
        grid = (m // tm, n // tn, nk)
        out_spec = pl.BlockSpec((tm, tn), lambda i, j, k: (i, j))
        scratch = [pltpu.VMEM((tm, tn), F32)]
        sem = ("parallel", "parallel", "arbitrary")
    return pl.pallas_call(
        functools.partial(_mm_body, n_a=n_a, nk=nk),
        out_shape=jax.ShapeDtypeStruct((m, n), out_dtype),
        grid=grid,
        in_specs=in_specs,
        out_specs=out_spec,
        scratch_shapes=scratch,
        compiler_params=_params(sem),
        name=name,
    )(*a_list, *b_list)


def _glu_body(a_ref, wg_ref, wu_ref, o_ref):
    a = a_ref[...]
    g = _dot(a, wg_ref[...])
    u = _dot(a, wu_ref[...])
    o_ref[...] = (g * _sigmoid(g) * u).astype(o_ref.dtype)


def _glu(a, wg, wu, m, tm, tn):
    n_e, kdim, n = wg.shape
    nj = n // tn
    w_spec = pl.BlockSpec((None, kdim, tn), lambda i, j: (j // nj, 0, j % nj))
    return pl.pallas_call(
        _glu_body,
        out_shape=jax.ShapeDtypeStruct((m, n_e * n), BF16),
        grid=(m // tm, n_e * nj),
        in_specs=[pl.BlockSpec((tm, kdim), lambda i, j: (i, 0)), w_spec, w_spec],
        out_specs=pl.BlockSpec((tm, tn), lambda i, j: (i, j)),
        compiler_params=_params(("parallel", "parallel")),
        name="glu",
    )(a, wg, wu)


def _router_body(u_ref, w_ref, o_ref, sel_ref, *, n_experts):
    lo, hi = _unpack_halves(u_ref[...])
    half = lo.shape[1]
    logits = _dot(lo, w_ref[pl.ds(0, half), :]) + _dot(hi, w_ref[pl.ds(half, half), :])
    lane = lax.broadcasted_iota(jnp.int32, logits.shape, 1)
    l1 = jnp.where(lane < n_experts, logits, NEG)
    m1 = jnp.max(l1, axis=-1, keepdims=True)
    i1 = jnp.min(jnp.where(l1 == m1, lane, LANES), axis=-1, keepdims=True)
    l2 = jnp.where(lane == i1, NEG, l1)
    m2 = jnp.max(l2, axis=-1, keepdims=True)
    i2 = jnp.min(jnp.where(l2 == m2, lane, LANES), axis=-1, keepdims=True)
    e2 = jnp.exp(m2 - m1)
    w1 = 1.0 / (1.0 + e2)
    w2 = e2 / (1.0 + e2)
    is1, is2 = lane == i1, lane == i2
    o_ref[...] = jnp.where(is1, w1, 0.0) + jnp.where(is2, w2, 0.0)
    sel_ref[...] = jnp.where(is1, 1.0, jnp.where(is2, 1.0, 0.0)).astype(sel_ref.dtype)


def _router(u_packed, w_router_padded, m, n_experts):
    half = u_packed.shape[1]
    tm = _tile(m, 1024, ROW_TILE)
    out = pl.BlockSpec((tm, LANES), lambda i: (i, 0))
    return pl.pallas_call(
        functools.partial(_router_body, n_experts=n_experts),
        out_shape=[jax.ShapeDtypeStruct((m, LANES), F32), jax.ShapeDtypeStruct((m, LANES), BF16)],
        grid=(m // tm,),
        in_specs=[pl.BlockSpec((tm, half), lambda i: (i, 0)), pl.BlockSpec((2 * half, LANES), lambda i: (0, 0))],
        out_specs=[out, out],
        compiler_params=_params(("parallel",)),
        name="router",
    )(u_packed, w_router_padded)


def _rank_body(sel_ref, rank_ref, cnt_ref, carry_ref):
    @pl.when(pl.program_id(0) == 0)
    def _():
        carry_ref[...] = jnp.zeros_like(carry_ref)

    sel = sel_ref[...]
    n = sel.shape[0]
    t = lax.broadcasted_iota(jnp.int32, (n, n), 0)
    s = lax.broadcasted_iota(jnp.int32, (n, n), 1)
    before = jnp.where(s < t, 1.0, 0.0).astype(BF16)
    carry = carry_ref[...]
    rank_ref[...] = _dot(before, sel) + carry[0:1, :]
    carry = carry + jnp.sum(sel.astype(F32), axis=0, keepdims=True)
    carry_ref[...] = carry
    cnt_ref[...] = carry


def _rank(sel):
    m = sel.shape[0]
    tile = _tile(m, RANK_TILE, ROW_TILE)
    blk = pl.BlockSpec((tile, LANES), lambda i: (i, 0))
    return pl.pallas_call(
        _rank_body,
        out_shape=[jax.ShapeDtypeStruct((m, LANES), F32), jax.ShapeDtypeStruct((8, LANES), F32)],
        grid=(m // tile,),
        in_specs=[blk],
        out_specs=[blk, pl.BlockSpec((8, LANES), lambda i: (0, 0))],
        scratch_shapes=[pltpu.VMEM((8, LANES), F32)],
        compiler_params=_params(("arbitrary",)),
        name="moe_rank",
    )(sel)


def _slots_body(rank_ref, sel_ref, comb_ref, cnt_ref, pos_ref, wts_ref):
    cnt = cnt_ref[...]
    n_tiles = jnp.floor((cnt + (MOE_TILE - 1)) * (1.0 / MOE_TILE))
    a = lax.broadcasted_iota(jnp.int32, (LANES, LANES), 0)
    b = lax.broadcasted_iota(jnp.int32, (LANES, LANES), 1)
    earlier = jnp.where(a < b, 1.0, 0.0).astype(BF16)
    start = _dot(n_tiles.astype(BF16), earlier)[0:1, :] * MOE_TILE
    sel = sel_ref[...].astype(F32) > 0.5
    lane = lax.broadcasted_iota(jnp.int32, sel.shape, 1)
    lane_a = jnp.min(jnp.where(sel, lane, LANES), axis=-1, keepdims=True)
    lane_b = jnp.max(jnp.where(sel, lane, -1), axis=-1, keepdims=True)
    row = start + rank_ref[...]
    comb = comb_ref[...]

    def pick(x, which):
        return jnp.sum(jnp.where(lane == which, x, 0.0), axis=-1, keepdims=True)

    pos_ref[...] = jnp.where(lane == 0, pick(row, lane_a), jnp.where(lane == 1, pick(row, lane_b), 0.0)).astype(jnp.int32)
    wts_ref[...] = jnp.where(lane == 0, pick(comb, lane_a), jnp.where(lane == 1, pick(comb, lane_b), 0.0))


def _slots(rank, sel, comb, counts):
    m = rank.shape[0]
    tm = _tile(m, 1024, ROW_TILE)
    blk = pl.BlockSpec((tm, LANES), lambda i: (i, 0))
    return pl.pallas_call(
        _slots_body,
        out_shape=[jax.ShapeDtypeStruct((m, LANES), jnp.int32), jax.ShapeDtypeStruct((m, LANES), F32)],
        grid=(m // tm,),
        in_specs=[blk, blk, blk, pl.BlockSpec((8, LANES), lambda i: (0, 0))],
        out_specs=[blk, blk],
        compiler_params=_params(("parallel",)),
        name="moe_slots",
    )(rank, sel, comb, counts)


def _row_copy(src_ref, src_row, dst_ref, dst_row, sem):
    return pltpu.make_async_copy(src_ref.at[pl.ds(src_row, 1)], dst_ref.at[pl.ds(dst_row, 1)], sem)


def _invert_body(pa_ref, pb_ref, src_ref, *, n_tokens):
    def clear(r, carry):
        src_ref[r] = 0
        return carry

    lax.fori_loop(0, src_ref.shape[0], clear, 0)

    def scatter(t, carry):
        src_ref[pa_ref[t]] = t
        src_ref[pb_ref[t]] = t
        return carry

    lax.fori_loop(0, n_tokens, scatter, 0)


def _invert(pos_a, pos_b, n_sorted):
    grid_spec = pltpu.PrefetchScalarGridSpec(
        num_scalar_prefetch=2, grid=(1,), in_specs=[],
        out_specs=pl.BlockSpec(memory_space=pltpu.SMEM))
    return pl.pallas_call(
        functools.partial(_invert_body, n_tokens=pos_a.shape[0]),
        out_shape=jax.ShapeDtypeStruct((n_sorted,), jnp.int32),
        grid_spec=grid_spec,
        compiler_params=_params(("arbitrary",)),
        name="moe_invert",
    )(pos_a, pos_b)


def _dispatch_body(src_ref, u_ref, xs_ref, sem):
    i = pl.program_id(0)
    tile = xs_ref.shape[0]

    def start(r, carry):
        _row_copy(u_ref, src_ref[i * tile + r], xs_ref, r, sem).start()
        return carry

    lax.fori_loop(0, tile, start, 0)

    def wait(r, carry):
        _row_copy(u_ref, 0, xs_ref, 0, sem).wait()
        return carry

    lax.fori_loop(0, tile, wait, 0)


def _dispatch(src, u_packed):
    n_sorted = src.shape[0]
    half = u_packed.shape[1]
    grid_spec = pltpu.PrefetchScalarGridSpec(
        num_scalar_prefetch=1, grid=(n_sorted // MOE_TILE,),
        in_specs=[pl.BlockSpec(memory_space=pl.ANY)],
        out_specs=pl.BlockSpec((MOE_TILE, half), lambda i, src: (i, 0)),
        scratch_shapes=[pltpu.SemaphoreType.DMA(())])
    return pl.pallas_call(
        _dispatch_body,
        out_shape=jax.ShapeDtypeStruct((n_sorted, half), U32),
        grid_spec=grid_spec,
        compiler_params=_params(("arbitrary",)),
        name="moe_dispatch",
    )(src, u_packed)


def _expert_glu_body(te_ref, x_ref, wg_ref, wu_ref, o_ref):
    del te_ref
    lo, hi = _unpack_halves(x_ref[...])
    half = lo.shape[1]
    g = _dot(lo, wg_ref[pl.ds(0, half), :]) + _dot(hi, wg_ref[pl.ds(half, half), :])
    u = _dot(lo, wu_ref[pl.ds(0, half), :]) + _dot(hi, wu_ref[pl.ds(half, half), :])
    o_ref[...] = (g * _sigmoid(g) * u).astype(o_ref.dtype)


def _expert_glu(tile_expert, xs, wg, wu, tn):
    n_sorted, half = xs.shape
    _, kdim, n = wg.shape
    w_spec = pl.BlockSpec((None, kdim, tn), lambda j, i, te: (te[i], 0, j))
    grid_spec = pltpu.PrefetchScalarGridSpec(
        num_scalar_prefetch=1, grid=(n // tn, n_sorted // MOE_TILE),
        in_specs=[pl.BlockSpec((MOE_TILE, half), lambda j, i, te: (i, 0)), w_spec, w_spec],
        out_specs=pl.BlockSpec((MOE_TILE, tn), lambda j, i, te: (i, j)))
    return pl.pallas_call(
        _expert_glu_body,
        out_shape=jax.ShapeDtypeStruct((n_sorted, n), BF16),
        grid_spec=grid_spec,
        compiler_params=_params(("parallel", "arbitrary")),
        name="moe_glu",
    )(tile_expert, xs, wg, wu)


def _expert_down_body(te_ref, h_ref, w_ref, o_ref):
    del te_ref
    o_ref[...] = _dot(h_ref[...], w_ref[...])


def _expert_down(tile_expert, hs, wd, tn):
    n_sorted, kdim = hs.shape
    n = wd.shape[2]
    grid_spec = pltpu.PrefetchScalarGridSpec(
        num_scalar_prefetch=1, grid=(n // tn, n_sorted // MOE_TILE),
        in_specs=[pl.BlockSpec((MOE_TILE, kdim), lambda j, i, te: (i, 0)),
                  pl.BlockSpec((None, kdim, tn), lambda j, i, te: (te[i], 0, j))],
        out_specs=pl.BlockSpec((MOE_TILE, tn), lambda j, i, te: (i, j)))
    return pl.pallas_call(
        _expert_down_body,
        out_shape=jax.ShapeDtypeStruct((n_sorted, n), F32),
        grid_spec=grid_spec,
        compiler_params=_params(("parallel", "arbitrary")),
        name="moe_down",
    )(tile_expert, hs, wd)


def _combine_post_body(pa_ref, pb_ref, os_ref, h_ref, wts_ref, gp_ref, gate_ref, ho_ref, buf, sems):
    i = pl.program_id(0)
    n = pl.num_programs(0)
    tile = h_ref.shape[0]
    slot = i % 2

    def issue(tile_idx, dst_slot):
        def one(t, carry):
            tok = tile_idx * tile + t
            _row_copy(os_ref, pa_ref[tok], buf.at[dst_slot, 0], t, sems.at[dst_slot]).start()
            _row_copy(os_ref, pb_ref[tok], buf.at[dst_slot, 1], t, sems.at[dst_slot]).start()
            return carry
        lax.fori_loop(0, tile, one, 0)

    @pl.when(i == 0)
    def _():
        issue(0, 0)

    @pl.when(i + 1 < n)
    def _():
        issue(i + 1, 1 - slot)

    def wait_one(t, carry):
        _row_copy(os_ref, 0, buf.at[slot, 0], 0, sems.at[slot]).wait()
        return carry

    lax.fori_loop(0, 2 * tile, wait_one, 0)
    wts = wts_ref[...]
    y = wts[:, 0:1] * buf[slot, 0] + wts[:, 1:2] * buf[slot, 1]
    ho_ref[...] = h_ref[...] + gate_ref[...] * _rms(y, gp_ref[...])


def _combine_post(pos_a, pos_b, os, h, wts, g_post, gate, group, n_rows):
    d = h.shape[1]
    row = lambda i, pa, pb: (i, 0)
    grid_spec = pltpu.PrefetchScalarGridSpec(
        num_scalar_prefetch=2, grid=(n_rows // ROW_TILE,),
        in_specs=[pl.BlockSpec(memory_space=pl.ANY),
                  pl.BlockSpec((ROW_TILE, d), row),
                  pl.BlockSpec((ROW_TILE, LANES), row),
                  pl.BlockSpec((1, d), lambda i, pa, pb: (0, 0)),
                  pl.BlockSpec((None, 1, d), lambda i, pa, pb: (group(i), 0, 0))],
        out_specs=pl.BlockSpec((ROW_TILE, d), row),
        scratch_shapes=[pltpu.VMEM((2, 2, ROW_TILE, d), F32), pltpu.SemaphoreType.DMA((2,))])
    return pl.pallas_call(
        _combine_post_body,
        out_shape=jax.ShapeDtypeStruct((n_rows, d), F32),
        grid_spec=grid_spec,
        compiler_params=_params(("arbitrary",)),
        name="moe_combine_post",
    )(pos_a, pos_b, os, h, wts, g_post, gate)


def _split3(x):
    hi = x.astype(BF16)
    r1 = x - hi.astype(F32)
    mid = r1.astype(BF16)
    lo = (r1 - mid.astype(F32)).astype(BF16)
    return hi, mid, lo


def _gates_body(g_ref, b_ref, o_ref, *, n_heads):
    pre = g_ref[...] + b_ref[...]
    log_f = jnp.minimum(pre, 0.0) - jnp.log(1.0 + jnp.exp(-jnp.abs(pre)))
    lc = pre.shape[0]
    t = lax.broadcasted_iota(jnp.int32, (lc, lc), 0)
    s = lax.broadcasted_iota(jnp.int32, (lc, lc), 1)
    tri_f = jnp.where(s <= t, 1.0, 0.0).astype(BF16)
    tri_b = jnp.where(s >= t, 1.0, 0.0).astype(BF16)
    hi, mid, lo = _split3(log_f)
    cs_f = _dot(tri_f, hi) + _dot(tri_f, mid) + _dot(tri_f, lo)
    cs_b = _dot(tri_b, hi) + _dot(tri_b, mid) + _dot(tri_b, lo)
    lane = lax.broadcasted_iota(jnp.int32, pre.shape, 1)
    kind = lane // n_heads
    a_f = pre - pltpu.roll(cs_f, LANES - n_heads, 1)
    a_b = pre - pltpu.roll(cs_b, LANES - n_heads, 1)
    o_ref[...] = jnp.where(kind == 0, a_f, jnp.where(kind == 1, cs_f, jnp.where(kind == 2, a_b, cs_b)))


def _gates(g_raw, bias_row, n_heads):
    t_all = g_raw.shape[0]
    blk = pl.BlockSpec((MLSTM_CHUNK, LANES), lambda i: (i, 0))
    return pl.pallas_call(
        functools.partial(_gates_body, n_heads=n_heads),
        out_shape=jax.ShapeDtypeStruct((t_all, LANES), F32),
        grid=(t_all // MLSTM_CHUNK,),
        in_specs=[blk, pl.BlockSpec((1, LANES), lambda i: (0, 0))],
        out_specs=blk,
        compiler_params=_params(("parallel",)),
        name="mlstm_gates",
    )(g_raw, bias_row)


STATE_ROWS = HEAD_DIM + 16


def _mlstm_chunk(q, k, vte, a_col, a_row, b_row, state, rev):
    ct, m_st = state
    lc = q.shape[0]
    b_last = b_row[:, 0:1] if rev else b_row[:, lc - 1:lc]
    m_new = jnp.maximum(b_last + m_st, b_last + jnp.max(a_row, axis=-1, keepdims=True))
    decay = jnp.exp(b_last + m_st - m_new)
    w_row = jnp.exp(b_last + a_row - m_new)
    lhs = (vte.astype(F32) * w_row).astype(BF16)
    ct_new = decay * ct + _dot(lhs, k)

    j = lax.broadcasted_iota(jnp.int32, (lc, lc), 0)
    t = lax.broadcasted_iota(jnp.int32, (lc, lc), 1)
    seen = (j >= t) if rev else (j <= t)
    dmat = jnp.where(seen, a_col + b_row, NEG)
    m_inter = b_row + m_st
    m_t = jnp.maximum(m_inter, jnp.max(dmat, axis=0, keepdims=True))
    s = _dot_nt(k, q) * jnp.exp(dmat - m_t)
    inter = jnp.exp(m_inter - m_t)
    from_state = _dot_nt(ct.astype(BF16), q)
    num = _dot(vte, s.astype(BF16))[:HEAD_DIM] + inter * from_state[:HEAD_DIM]
    den = jnp.sum(s, axis=0, keepdims=True) + inter * from_state[HEAD_DIM:HEAD_DIM + 1]
    h = num / jnp.maximum(jnp.abs(den), jnp.exp(-m_t))
    return h, (ct_new, m_new)


def _mlstm_body(qc_ref, kc_ref, vtc_ref, oc_ref, ql_ref, kl_ref, vtl_ref, ol_ref,
                acf_ref, acb_ref, grc_ref, alf_ref, alb_ref, grl_ref, cos_ref, sin_ref, ng_ref, *rest, with_ctx_out):
    if with_ctx_out:
        yl_ref, yc_ref, qs_ref, ks_ref, vte_ref, hf_ref, hb_ref = rest
    else:
        yl_ref, qs_ref, ks_ref, vte_ref, hf_ref, hb_ref = rest
    lc = MLSTM_CHUNK
    n_lat = ql_ref.shape[0] // lc
    n_ctx = qc_ref.shape[0] // lc
    scale = HEAD_DIM ** -0.5
    lane = lax.broadcasted_iota(jnp.int32, (lc, HEAD_DIM), 1)
    first_quarter = (lane % (HEAD_DIM // 2)) < (HEAD_DIM // 4)
    pad_row = lax.broadcasted_iota(jnp.int32, (STATE_ROWS - HEAD_DIM, lc), 0)
    ones_pad = jnp.where(pad_row == 0, 1.0, 0.0).astype(BF16)

    def rope(x, rows):
        partner = jnp.where(first_quarter, pltpu.roll(x, HEAD_DIM - HEAD_DIM // 4, 1), pltpu.roll(x, HEAD_DIM // 4, 1))
        return x * cos_ref[rows, :] + partner * sin_ref[rows, :]

    def prep_chunk(i, carry):
        rows = pl.ds(pl.multiple_of(i * lc, lc), lc)
        qs_ref[rows, :] = rope(ql_ref[rows, :].astype(F32) * scale, rows).astype(BF16)
        ks_ref[rows, :] = rope(kl_ref[rows, :].astype(F32), rows).astype(BF16)
        vte_ref[pl.ds(0, HEAD_DIM), rows] = vtl_ref[:, rows]
        vte_ref[pl.ds(HEAD_DIM, STATE_ROWS - HEAD_DIM), rows] = ones_pad
        return carry

    lax.fori_loop(0, n_lat, prep_chunk, 0)

    ng = ng_ref[...]

    def finish(h_t, o_gate):
        y_t = h_t * lax.rsqrt(jnp.mean(h_t * h_t, axis=0, keepdims=True) + EPS)
        return (_sigmoid(o_gate.astype(F32)) * (y_t.T * ng)).astype(BF16)

    zero = (jnp.zeros((STATE_ROWS, HEAD_DIM), F32), jnp.zeros((1, 1), F32))
    a_cols_c, a_cols_l = (acf_ref, acb_ref), (alf_ref, alb_ref)

    st = [zero, zero]
    for i in range(n_ctx):
        for d in range(2):
            ci = i if d == 0 else n_ctx - 1 - i
            rows = pl.ds(ci * lc, lc)
            q = (qc_ref[rows, :].astype(F32) * scale).astype(BF16)
            vte = jnp.concatenate([vtc_ref[:, rows], ones_pad], axis=0)
            h, st[d] = _mlstm_chunk(q, kc_ref[rows, :], vte, a_cols_c[d][rows, :], grc_ref[pl.ds(2 * d, 1), rows],
                                    grc_ref[pl.ds(2 * d + 1, 1), rows], st[d], d == 1)
            if with_ctx_out:
                (hf_ref if d == 0 else hb_ref)[:, rows] = h
    if with_ctx_out:
        for i in range(n_ctx):
            rows = pl.ds(i * lc, lc)
            yc_ref[rows, :] = finish(hf_ref[:, rows] + hb_ref[:, rows], oc_ref[rows, :])

    def lat_step(i, carry):
        out = []
        for d, st_d in enumerate(carry):
            ci = i if d == 0 else n_lat - 1 - i
            rows = pl.ds(pl.multiple_of(ci * lc, lc), lc)
            h, st_d = _mlstm_chunk(qs_ref[rows, :], ks_ref[rows, :], vte_ref[:, rows], a_cols_l[d][rows, :],
                                   grl_ref[pl.ds(2 * d, 1), rows], grl_ref[pl.ds(2 * d + 1, 1), rows], st_d, d == 1)
            (hf_ref if d == 0 else hb_ref)[:, rows] = h
            out.append(st_d)
        return tuple(out)

    lax.fori_loop(0, n_lat, lat_step, (st[0], st[1]))

    def finish_chunk(i, carry):
        rows = pl.ds(pl.multiple_of(i * lc, lc), lc)
        yl_ref[rows, :] = finish(hf_ref[:, rows] + hb_ref[:, rows], ol_ref[rows, :])
        return carry

    lax.fori_loop(0, n_lat, finish_chunk, 0)


def _mlstm(p, v_t, a_col, g_row, cos, sin_signed, norm_g, n_batch, seq, ctx_len, n_heads, with_ctx_out):
    ctx_blk0 = n_batch * seq // ctx_len

    def col(kind, lat):
        if lat:
            return pl.BlockSpec((seq, HEAD_DIM), lambda b, h: (b, kind * n_heads + h))
        return pl.BlockSpec((ctx_len, HEAD_DIM), lambda b, h: (ctx_blk0 + b, kind * n_heads + h))

    def a_spec(d, lat):
        if lat:
            return pl.BlockSpec((None, None, seq, 1), lambda b, h: (h, d, b, 0))
        return pl.BlockSpec((None, None, ctx_len, 1), lambda b, h: (h, d, ctx_blk0 + b, 0))

    in_specs = [
        col(0, False), col(1, False), pl.BlockSpec((HEAD_DIM, ctx_len), lambda b, h: (h, ctx_blk0 + b)), col(3, False),
        col(0, True), col(1, True), pl.BlockSpec((HEAD_DIM, seq), lambda b, h: (h, b)), col(3, True),
        a_spec(0, False), a_spec(1, False), pl.BlockSpec((None, 4, ctx_len), lambda b, h: (h, 0, ctx_blk0 + b)),
        a_spec(0, True), a_spec(1, True), pl.BlockSpec((None, 4, seq), lambda b, h: (h, 0, b)),
        pl.BlockSpec((seq, HEAD_DIM), lambda b, h: (0, 0)),
        pl.BlockSpec((seq, HEAD_DIM), lambda b, h: (0, 0)),
        pl.BlockSpec((None, 1, HEAD_DIM), lambda b, h: (h, 0, 0)),
    ]
    out_shape = [jax.ShapeDtypeStruct((n_batch * seq, n_heads * HEAD_DIM), BF16)]
    out_specs = [pl.BlockSpec((seq, HEAD_DIM), lambda b, h: (b, h))]
    if with_ctx_out:
        out_shape.append(jax.ShapeDtypeStruct((n_batch * ctx_len, n_heads * HEAD_DIM), BF16))
        out_specs.append(pl.BlockSpec((ctx_len, HEAD_DIM), lambda b, h: (b, h)))
    return pl.pallas_call(
        functools.partial(_mlstm_body, with_ctx_out=with_ctx_out),
        out_shape=out_shape,
        grid=(n_batch, n_heads),
        in_specs=in_specs,
        out_specs=out_specs,
        scratch_shapes=[pltpu.VMEM((seq, HEAD_DIM), BF16), pltpu.VMEM((seq, HEAD_DIM), BF16),
                        pltpu.VMEM((STATE_ROWS, seq), BF16),
                        pltpu.VMEM((HEAD_DIM, seq), F32), pltpu.VMEM((HEAD_DIM, seq), F32)],
        compiler_params=_params(("parallel", "parallel")),
        name="mlstm",
    )(p, p, v_t, p, p, p, v_t, p, a_col, a_col, g_row, a_col, a_col, g_row, cos, sin_signed, norm_g)


def _conv_body(ap_ref, gp_ref, a_ref, g_ref, an_ref, gn_ref, w_ref, b_ref, ng_ref, o_ref, u_ref, c_ref,
               *, n_lat_tiles, lat_tiles_per_seq, width):
    i = pl.program_id(0)
    is_lat = i < n_lat_tiles
    pos = i % lat_tiles_per_seq
    first = jnp.logical_or(jnp.logical_not(is_lat), pos == 0)
    last = jnp.logical_or(jnp.logical_not(is_lat), pos == lat_tiles_per_seq - 1)

    def glu(x_ref, y_ref):
        return x_ref[...].astype(F32) * _sigmoid(y_ref[...].astype(F32))

    tile = a_ref.shape[0]
    u_ref[pl.ds(0, HALO), :] = jnp.where(first, 0.0, 1.0) * glu(ap_ref, gp_ref)
    u_ref[pl.ds(HALO, tile), :] = glu(a_ref, g_ref)
    u_ref[pl.ds(HALO + tile, HALO), :] = jnp.where(last, 0.0, 1.0) * glu(an_ref, gn_ref)

    strip, cblk = 32, 256
    n_ch = a_ref.shape[1]
    cblk = min(cblk, n_ch)
    base = HALO - width // 2

    for r0 in range(0, tile, strip):
        for cb in range(n_ch // cblk):
            cols = pl.ds(cb * cblk, cblk)
            acc = jnp.zeros((strip, cblk), F32)
            for k in range(width):
                acc += w_ref[pl.ds(k, 1), cols] * u_ref[pl.ds(r0 + base + k, strip), cols]
            c_ref[pl.ds(r0, strip), cols] = acc
    y = _rms(c_ref[...] + b_ref[...], ng_ref[...])
    o_ref[...] = (y * _sigmoid(y)).astype(o_ref.dtype)


def _conv(p, conv_w, conv_b, conv_norm, a_col_blk, n_rows, n_lat_tiles, lat_tiles_per_seq):
    width, n_ch = conv_w.shape
    assert width // 2 <= HALO
    per = ROW_TILE // HALO
    last_halo = p.shape[0] // HALO - 1

    def cur(c):
        return pl.BlockSpec((ROW_TILE, n_ch), lambda i: (i, c))

    def prev(c):
        return pl.BlockSpec((HALO, n_ch), lambda i: (jnp.maximum(i * per - 1, 0), c))

    def nxt(c):
        return pl.BlockSpec((HALO, n_ch), lambda i: (jnp.minimum((i + 1) * per, last_halo), c))

    vec = pl.BlockSpec((1, n_ch), lambda i: (0, 0))
    a, g = a_col_blk, a_col_blk + 1
    return pl.pallas_call(
        functools.partial(_conv_body, n_lat_tiles=n_lat_tiles, lat_tiles_per_seq=lat_tiles_per_seq, width=width),
        out_shape=jax.ShapeDtypeStruct((n_rows, n_ch), BF16),
        grid=(n_rows // ROW_TILE,),
        in_specs=[prev(a), prev(g), cur(a), cur(g), nxt(a), nxt(g),
                  pl.BlockSpec((width, n_ch), lambda i: (0, 0)), vec, vec],
        out_specs=pl.BlockSpec((ROW_TILE, n_ch), lambda i: (i, 0)),
        scratch_shapes=[pltpu.VMEM((ROW_TILE + 2 * HALO, n_ch), F32), pltpu.VMEM((ROW_TILE, n_ch), F32)],
        compiler_params=_params(("parallel",)),
        name="conformer_conv",
    )(p, p, p, p, p, p, conv_w, conv_b, conv_norm)


def _na_plan(rows):
    assert rows % NA_GROUP == 0 and rows >= NA_UNION
    plans = []
    for g in range(rows // NA_GROUP):
        union_start = min(max(g * NA_GROUP - NA_ROWS // 2, 0), rows - NA_UNION)
        plan = []
        for r in range(g * NA_GROUP, (g + 1) * NA_GROUP):
            row_start = min(max(r - NA_ROWS // 2, 0), rows - NA_ROWS)
            offset = row_start - union_start
            first_tile = row_start - r + NA_ROWS - 1
            assert 0 <= offset <= NA_UNION - NA_ROWS and first_tile - offset + NA_PAD_TILES >= 0
            plan.append((first_tile - offset + NA_PAD_TILES, offset))
        plans.append(tuple(plan))
    assert all(p == plans[1] for p in plans[1:-1])
    return plans[0], plans[1], plans[-1]


def _na_body(*refs, rows, with_ctx_out):
    if with_ctx_out:
        q_ref, k_ref, v_ref, kc_ref, vc_ref, bias_ref, qc_ref, o_ref, oc_ref = refs
    else:
        q_ref, k_ref, v_ref, kc_ref, vc_ref, bias_ref, o_ref = refs
    scale = HEAD_DIM ** -0.5
    kc = kc_ref[...]
    vc = vc_ref[...]
    n_groups = rows // NA_GROUP

    def row_group(g, carry):
        rows_g = pl.ds(pl.multiple_of(g * (NA_GROUP * GRID_W), NA_GROUP * GRID_W), NA_GROUP * GRID_W)
        union_start = jnp.clip(g * NA_GROUP - NA_ROWS // 2, 0, rows - NA_UNION)
        win = pl.ds(pl.multiple_of(union_start * GRID_W, GRID_W), NA_UNION * GRID_W)
        variant = jnp.where(g == 0, 0, jnp.where(g == n_groups - 1, 2, 1))
        q_g = q_ref[rows_g, :]
        s_win = _dot_nt(q_g, k_ref[win, :]) * scale + bias_ref[variant]
        s_ctx = _dot_nt(q_g, kc) * scale
        m = jnp.maximum(jnp.max(s_win, axis=-1, keepdims=True), jnp.max(s_ctx, axis=-1, keepdims=True))
        p_win = jnp.exp(s_win - m)
        p_ctx = jnp.exp(s_ctx - m)
        denom = jnp.sum(p_win, axis=-1, keepdims=True) + jnp.sum(p_ctx, axis=-1, keepdims=True)
        out = (_dot(p_win.astype(BF16), v_ref[win, :]) + _dot(p_ctx.astype(BF16), vc)) / denom
        o_ref[rows_g, :] = out.astype(o_ref.dtype)
        return carry

    lax.fori_loop(0, n_groups, row_group, 0)

    if with_ctx_out:
        s = _dot_nt(qc_ref[...], kc) * scale
        p = jnp.exp(s - jnp.max(s, axis=-1, keepdims=True))
        oc_ref[...] = (_dot(p.astype(BF16), vc) / jnp.sum(p, axis=-1, keepdims=True)).astype(oc_ref.dtype)


def _na(p, bias, col0, n_batch, seq, ctx_len, n_heads, with_ctx_out):
    ctx_blk0 = n_batch * seq // ctx_len
    rows = seq // GRID_W

    def lat(kind):
        return pl.BlockSpec((seq, HEAD_DIM), lambda b, h: (b, col0 + kind * n_heads + h))

    def ctx(kind):
        return pl.BlockSpec((ctx_len, HEAD_DIM), lambda b, h: (ctx_blk0 + b, col0 + kind * n_heads + h))

    in_specs = [lat(0), lat(1), lat(2), ctx(1), ctx(2),
                pl.BlockSpec((None,) + bias.shape[1:], lambda b, h: (h, 0, 0, 0))]
    args = [p, p, p, p, p, bias]
    out_shape = [jax.ShapeDtypeStruct((n_batch * seq, n_heads * HEAD_DIM), BF16)]
    out_specs = [pl.BlockSpec((seq, HEAD_DIM), lambda b, h: (b, h))]
    if with_ctx_out:
        in_specs.append(ctx(0))
        args.append(p)
        out_shape.append(jax.ShapeDtypeStruct((n_batch * ctx_len, n_heads * HEAD_DIM), BF16))
        out_specs.append(pl.BlockSpec((ctx_len, HEAD_DIM), lambda b, h: (b, h)))
    return pl.pallas_call(
        functools.partial(_na_body, rows=rows, with_ctx_out=with_ctx_out),
        out_shape=out_shape,
        grid=(n_batch, n_heads),
        in_specs=in_specs,
        out_specs=out_specs,
        compiler_params=_params(("parallel", "parallel")),
        name="neighbourhood_attention",
    )(*args)


def _na_bias_body(r_ref, o_ref, *, plans):
    n_taps = 2 * NA_COLS - 1
    width = r_ref.shape[1]
    q = lax.broadcasted_iota(jnp.int32, (GRID_W, width), 0)
    k = lax.broadcasted_iota(jnp.int32, (GRID_W, width), 1) % GRID_W
    tap = k - q + (NA_COLS - 1)
    strip = jnp.zeros((GRID_W, width), F32)
    for j in range(n_taps):
        strip = jnp.where(tap == j, r_ref[pl.ds(j, 1), :], strip)
    col_start = jnp.clip(q - NA_COLS // 2, 0, GRID_W - NA_COLS)
    strip = jnp.where(k >= col_start, jnp.where(k < col_start + NA_COLS, strip, NEG), NEG)
    win = NA_UNION * GRID_W
    key_row = lax.broadcasted_iota(jnp.int32, (GRID_W, win), 1) // GRID_W
    for v, plan in enumerate(plans):
        for i, (shift, offset) in enumerate(plan):
            piece = strip[:, shift * GRID_W:shift * GRID_W + win]
            piece = jnp.where(key_row >= offset, jnp.where(key_row < offset + NA_ROWS, piece, NEG), NEG)
            o_ref[v, pl.ds(i * GRID_W, GRID_W), :] = piece


def _na_bias_table(rpb, rows):
    n_heads, n_dr, n_taps = rpb.shape
    assert n_dr == 2 * NA_ROWS - 1 and n_taps == 2 * NA_COLS - 1
    plans = _na_plan(rows)
    n_tiles = max(shift for plan in plans for shift, _ in plan) + NA_UNION
    width = -(-(n_tiles * GRID_W) // LANES) * LANES
    spread = jnp.repeat(rpb.transpose(0, 2, 1), GRID_W, axis=2)
    lead = NA_PAD_TILES * GRID_W
    spread = jnp.pad(spread, ((0, 0), (0, 32 - n_taps), (lead, width - lead - n_dr * GRID_W)))
    out_blk = (len(plans), NA_GROUP * GRID_W, NA_UNION * GRID_W)
    return pl.pallas_call(
        functools.partial(_na_bias_body, plans=plans),
        out_shape=jax.ShapeDtypeStruct((n_heads,) + out_blk, F32),
        grid=(n_heads,),
        in_specs=[pl.BlockSpec((None, 32, width), lambda h: (h, 0, 0))],
        out_specs=pl.BlockSpec((None,) + out_blk, lambda h: (h, 0, 0, 0)),
        compiler_params=_params(("parallel",)),
        name="na_bias",
    )(spread)


def _rope_tables(length):
    t = jnp.arange(length)
    n_freq = HEAD_DIM // 4
    freqs = ROPE_BASE ** (-jnp.arange(n_freq, dtype=F32) / n_freq)
    ang_r = (t // GRID_W).astype(F32)[:, None] * freqs
    ang_c = (t % GRID_W).astype(F32)[:, None] * freqs
    ang = jnp.concatenate([ang_r, ang_r, ang_c, ang_c], axis=-1)
    sign = jnp.concatenate([-jnp.ones(n_freq), jnp.ones(n_freq), -jnp.ones(n_freq), jnp.ones(n_freq)])
    return jnp.cos(ang), jnp.sin(ang) * sign


def kernel(x, c, ctx, c_ctx, w_mod, b_mod, g_norm, w_in, w_out, mlstm_gate_bias, mlstm_norm, conv_w, conv_b,
           conv_norm, na_rpb, ffn_w_gate, ffn_w_up, ffn_w_down, router_w, moe_w_gate, moe_w_up, moe_w_down):
    n_batch, seq, d = x.shape
    ctx_len = ctx.shape[1]
    depth = w_mod.shape[0]
    m_width = mlstm_norm.shape[1]
    c_width = conv_w.shape[2]
    n_width = d - m_width - c_width
    mh, nh = m_width // HEAD_DIM, n_width // HEAD_DIM
    n_experts = router_w.shape[2]
    t_lat, t_ctx = n_batch * seq, n_batch * ctx_len
    t_all = t_lat + t_ctx
    assert seq % MLSTM_CHUNK == 0 and ctx_len % MLSTM_CHUNK == 0 and ctx_len == ROW_TILE
    assert t_lat % ctx_len == 0 and (4 * m_width) % c_width == 0 and 4 * mh <= LANES
    n_lat_tiles, lat_tiles_per_seq = t_lat // ROW_TILE, seq // ROW_TILE
    group = _group_map(n_lat_tiles, lat_tiles_per_seq, n_batch)

    h = jnp.concatenate([x.reshape(t_lat, d), ctx.reshape(t_ctx, d)], axis=0)
    cond = jnp.concatenate([c, jnp.broadcast_to(c_ctx[None], (8 - n_batch, d))], axis=0)
    mod = _modulation(cond, w_mod, b_mod)
    grp_rows = jnp.concatenate([jnp.arange(n_batch), jnp.full((n_batch,), n_batch)])
    cos, sin_signed = _rope_tables(seq)
    tm_all, tm_lat = _tile(t_all, 1024, ROW_TILE), _tile(t_lat, 1024, ROW_TILE)

    def mods(layer):
        m6 = mod[layer][grp_rows].reshape(2 * n_batch, N_MOD, 1, d)
        return [m6[:, k] for k in range(N_MOD)]

    gv = lambda layer, k: g_norm[layer, k].reshape(1, d)

    sh1, sc1, _, _, _, _ = mods(0)
    u = _normmod(h, gv(0, 0), sh1, sc1, group, t_all)

    for layer in range(depth):
        last = layer == depth - 1
        with_ctx_out = not last
        sh1, sc1, gt1, sh2, sc2, gt2 = mods(layer)
        rows_out = t_lat if last else t_all
        tm_out = tm_lat if last else tm_all

        w = w_in[layer]
        cut_g, cut_c = 4 * m_width, 4 * m_width + 4 * mh
        w_main = jnp.concatenate([w[:, :cut_g], w[:, cut_c:]], axis=1).astype(BF16)
        w_gate = jnp.pad(w[:, cut_g:cut_c], ((0, 0), (0, LANES - 4 * mh))).astype(BF16)
        n_main = w_main.shape[1]
        p = _matmul([u], [w_main], BF16, t_all, tm_all, _tile(n_main, 1280, 2 * LANES), name="w_in")
        g_raw = _matmul([u], [w_gate], F32, t_all, tm_all, LANES, name="w_in_gates")

        bias_row = jnp.pad(mlstm_gate_bias[layer].reshape(1, 4 * mh), ((0, 0), (0, LANES - 4 * mh)))
        g_tab = _gates(g_raw, bias_row, mh)[:, :4 * mh].reshape(t_all, 4, mh)
        g_row = g_tab.transpose(2, 1, 0)
        a_col = g_row[:, 0::2, :, None]
        v_t = p[:, 2 * m_width:3 * m_width].T
        ym = _mlstm(p, v_t, a_col, g_row, cos, sin_signed, mlstm_norm[layer].reshape(mh, 1, HEAD_DIM),
                    n_batch, seq, ctx_len, mh, with_ctx_out)
        yc = _conv(p, conv_w[layer], conv_b[layer].reshape(1, c_width), conv_norm[layer].reshape(1, c_width),
                   cut_g // c_width, rows_out, n_lat_tiles, lat_tiles_per_seq)
        bias = _na_bias_table(na_rpb[layer], seq // GRID_W)
        yn = _na(p, bias, (cut_g + 2 * c_width) // HEAD_DIM, n_batch, seq, ctx_len, nh, with_ctx_out)
        if with_ctx_out:
            ym, yn = jnp.concatenate(ym, axis=0), jnp.concatenate(yn, axis=0)
        else:
            ym, yn = ym[0], yn[0]

        wo = w_out[layer]
        wo_parts = [wo[:m_width].astype(BF16), wo[m_width:m_width + c_width].astype(BF16),
                    wo[m_width + c_width:].astype(BF16)]
        mix = _matmul([ym, yc, yn], wo_parts, F32, rows_out, tm_out, _tile(d, 1024, 2 * LANES), name="w_out")
        moe = layer % 2 == 1
        h, u2 = _post(h, mix, gv(layer, 1), gt1, group, rows_out, nxt=(gv(layer, 2), sh2, sc2), pack_next=moe)

        j = layer // 2
        if not moe:
            d_ff = ffn_w_gate.shape[2]
            hid = _glu(u2, ffn_w_gate[j][None].astype(BF16), ffn_w_up[j][None].astype(BF16),
                       rows_out, tm_out, _tile(d_ff, 512, 2 * LANES))
            f = _matmul([hid], [ffn_w_down[j].astype(BF16)], F32, rows_out, tm_out, _tile(d, 1024, 2 * LANES),
                        tk=_tile(d_ff, 2048, 2 * LANES), name="ffn_down")
        else:
            d_e = moe_w_gate.shape[3]
            w_r = jnp.pad(router_w[j], ((0, 0), (0, LANES - n_experts))).astype(BF16)
            comb, sel = _router(u2, w_r, rows_out, n_experts)
            rank, counts = _rank(sel)
            pos, wts = _slots(rank, sel, comb, counts)
            pos_a, pos_b = pos[:, 0], pos[:, 1]
            n_sorted = TOP_K * rows_out + n_experts * MOE_TILE
            tiles_per_expert = (counts[0, :n_experts].astype(jnp.int32) + MOE_TILE - 1) // MOE_TILE
            tile_end = jnp.cumsum(tiles_per_expert)
            tile_expert = jnp.minimum(
                jnp.sum(jnp.arange(n_sorted // MOE_TILE)[:, None] >= tile_end[None, :], axis=1), n_experts - 1
            ).astype(jnp.int32)
            xs = _dispatch(_invert(pos_a, pos_b, n_sorted), u2)
            hs = _expert_glu(tile_expert, xs, moe_w_gate[j].astype(BF16), moe_w_up[j].astype(BF16),
                             _tile(d_e, 1024, 2 * LANES))
            os = _expert_down(tile_expert, hs, moe_w_down[j].astype(BF16), _tile(d, 1024, 2 * LANES))

        if moe:
            assert last
            h = _combine_post(pos_a, pos_b, os, h, wts, gv(layer, 3), gt2, group, rows_out)
        elif last:
            (h,) = _post(h, f, gv(layer, 3), gt2, group, rows_out)
        else:
            nsh1, nsc1 = mods(layer + 1)[:2]
            h, u = _post(h, f, gv(layer, 3), gt2, group, rows_out, nxt=(gv(layer + 1, 0), nsh1, nsc1))

    return h[:t_lat].reshape(n_batch, seq, d)
```

```python
import functools

import jax
import jax.numpy as jnp
from jax import lax
from jax.experimental import pallas as pl
from jax.experimental.pallas import tpu as pltpu

HEAD_DIM = 128
GRID_W = 64
NA_ROWS = 8
NA_COLS = 16
NA_GROUP = 4
NA_UNION = NA_ROWS + NA_GROUP - 1
NA_PAD_TILES = NA_UNION - NA_ROWS
ROPE_BASE = 10000.0
N_MOD = 6
TOP_K = 2
EPS = 1e-6
MLSTM_CHUNK = 256
ROW_TILE = 256
HALO = 16
LANES = 128
SUBLANES = 8
NEG = -1e30
VMEM_LIMIT = 56 * 1024 * 1024
MOE_TILE = 256
RANK_TILE = 512
SCALAR_UNROLL = 8

F32 = jnp.float32
BF16 = jnp.bfloat16
U32 = jnp.uint32


def _params(sem):
    return pltpu.CompilerParams(dimension_semantics=sem, vmem_limit_bytes=VMEM_LIMIT)


def _tile(n, target, mult=LANES):
    if n <= target:
        return n
    for step in (mult, LANES, 8):
        for t in range((target // step) * step, 0, -step):
            if n % t == 0:
                return t
    raise ValueError((n, target, mult))


def _dot(a, b):
    return jnp.dot(a, b, preferred_element_type=F32)


def _dot_nt(a, b):
    return lax.dot_general(a, b, (((1,), (1,)), ((), ())), preferred_element_type=F32)


def _sigmoid(x):
    return 1.0 / (1.0 + jnp.exp(-x))


def _rms(x, g):
    return x * lax.rsqrt(jnp.mean(x * x, axis=-1, keepdims=True) + EPS) * g


def _mod_body(c_ref, w_ref, b_ref, o_ref):
    k = pl.program_id(2)
    c = c_ref[...]
    s = (c * _sigmoid(c)).astype(BF16)
    part = _dot(s, w_ref[...].astype(BF16))

    @pl.when(k == 0)
    def _():
        o_ref[...] = part + b_ref[...]

    @pl.when(k > 0)
    def _():
        o_ref[...] += part


def _modulation(cond, w_mod, b_mod):
    depth, d, n = w_mod.shape
    tn, tk = _tile(n, 2048), _tile(d, 1024)
    return pl.pallas_call(
        _mod_body,
        out_shape=jax.ShapeDtypeStruct((depth, 8, n), F32),
        grid=(depth, n // tn, d // tk),
        in_specs=[pl.BlockSpec((8, tk), lambda l, j, k: (0, k)),
                  pl.BlockSpec((None, tk, tn), lambda l, j, k: (l, k, j)),
                  pl.BlockSpec((None, 1, tn), lambda l, j, k: (l, 0, j))],
        out_specs=pl.BlockSpec((None, 8, tn), lambda l, j, k: (l, 0, j)),
        compiler_params=_params(("parallel", "parallel", "arbitrary")),
        name="modulation",
    )(cond, w_mod, b_mod.reshape(depth, 1, n))


def _group_map(n_lat_tiles, lat_tiles_per_seq, n_batch):
    def group(i):
        return jnp.where(i < n_lat_tiles, i // lat_tiles_per_seq, n_batch + (i - n_lat_tiles))
    return group


def _normmod_body(h_ref, g_ref, sh_ref, sc_ref, u_ref):
    y = _rms(h_ref[...], g_ref[...])
    u_ref[...] = (y * (1.0 + sc_ref[...]) + sh_ref[...]).astype(u_ref.dtype)


def _normmod(h, g, sh, sc, group, n_rows):
    d = h.shape[1]
    row = pl.BlockSpec((ROW_TILE, d), lambda i: (i, 0))
    vec = pl.BlockSpec((1, d), lambda i: (0, 0))
    per_group = pl.BlockSpec((None, 1, d), lambda i: (group(i), 0, 0))
    return pl.pallas_call(
        _normmod_body,
        out_shape=jax.ShapeDtypeStruct((n_rows, d), BF16),
        grid=(n_rows // ROW_TILE,),
        in_specs=[row, vec, per_group, per_group],
        out_specs=row,
        compiler_params=_params(("parallel",)),
        name="normmod",
    )(h, g, sh, sc)


def _pack_halves(x):
    half = x.shape[1] // 2
    bits = lax.bitcast_convert_type(x.astype(BF16).astype(F32), U32)
    return (bits[:, half:] & jnp.uint32(0xFFFF0000)) | lax.shift_right_logical(bits[:, :half], jnp.uint32(16))


def _unpack_halves(w):
    lo = lax.bitcast_convert_type(lax.shift_left(w, jnp.uint32(16)), F32).astype(BF16)
    hi = lax.bitcast_convert_type(w & jnp.uint32(0xFFFF0000), F32).astype(BF16)
    return lo, hi


def _post_body(h_ref, y_ref, gp_ref, gate_ref, *rest, with_next, pack_next):
    h = h_ref[...] + gate_ref[...] * _rms(y_ref[...], gp_ref[...])
    if with_next:
        gn_ref, sh_ref, sc_ref, ho_ref, u_ref = rest
        ho_ref[...] = h
        u = _rms(h, gn_ref[...]) * (1.0 + sc_ref[...]) + sh_ref[...]
        u_ref[...] = _pack_halves(u) if pack_next else u.astype(u_ref.dtype)
    else:
        (ho_ref,) = rest
        ho_ref[...] = h


def _post(h, y, g_post, gate, group, n_rows, nxt=None, pack_next=False):
    d = h.shape[1]
    row = pl.BlockSpec((ROW_TILE, d), lambda i: (i, 0))
    vec = pl.BlockSpec((1, d), lambda i: (0, 0))
    per_group = pl.BlockSpec((None, 1, d), lambda i: (group(i), 0, 0))
    in_specs = [row, row, vec, per_group]
    args = [h, y, g_post, gate]
    out_shape = [jax.ShapeDtypeStruct((n_rows, d), F32)]
    out_specs = [row]
    if nxt is not None:
        in_specs += [vec, per_group, per_group]
        args += list(nxt)
        if pack_next:
            out_shape.append(jax.ShapeDtypeStruct((n_rows, d // 2), U32))
            out_specs.append(pl.BlockSpec((ROW_TILE, d // 2), lambda i: (i, 0)))
        else:
            out_shape.append(jax.ShapeDtypeStruct((n_rows, d), BF16))
            out_specs.append(row)
    return pl.pallas_call(
        functools.partial(_post_body, with_next=nxt is not None, pack_next=pack_next),
        out_shape=out_shape,
        grid=(n_rows // ROW_TILE,),
        in_specs=in_specs,
        out_specs=out_specs,
        compiler_params=_params(("parallel",)),
        name="post",
    )(*args)


def _mm_body(*refs, n_a, nk):
    a_refs, b_refs, o_ref = refs[:n_a], refs[n_a:2 * n_a], refs[2 * n_a]
    part = _dot(a_refs[0][...], b_refs[0][...])
    for a_ref, b_ref in zip(a_refs[1:], b_refs[1:]):
        part += _dot(a_ref[...], b_ref[...])
    if nk == 1:
        o_ref[...] = part.astype(o_ref.dtype)
        return
    acc_ref = refs[2 * n_a + 1]
    k = pl.program_id(2)

    @pl.when(k == 0)
    def _():
        acc_ref[...] = part

    @pl.when(k > 0)
    def _():
        acc_ref[...] += part

    @pl.when(k == nk - 1)
    def _():
        o_ref[...] = acc_ref[...].astype(o_ref.dtype)


def _matmul(a_list, b_list, out_dtype, m, tm, tn, tk=None, name="matmul"):
    n = b_list[0].shape[1]
    n_a = len(a_list)
    if tk is None:
        nk = 1
        in_specs = ([pl.BlockSpec((tm, a.shape[1]), lambda i, j: (i, 0)) for a in a_list]
                    + [pl.BlockSpec((b.shape[0], tn), lambda i, j: (0, j)) for b in b_list])
        grid = (m // tm, n // tn)
        out_spec = pl.BlockSpec((tm, tn), lambda i, j: (i, j))
        scratch = []
        sem = ("parallel", "parallel")
    else:
        assert n_a == 1
        kdim = a_list[0].shape[1]
        nk = kdim // tk
        in_specs = [pl.BlockSpec((tm, tk), lambda i, j, k: (i, k)),
                    pl.BlockSpec((tk, tn), lambda i, j, k: (k, j))]
        grid = (m // tm, n // tn, nk)
        out_spec = pl.BlockSpec((tm, tn), lambda i, j, k: (i, j))
        scratch = [pltpu.VMEM((tm, tn), F32)]
        sem = ("parallel", "parallel", "arbitrary")
    return pl.pallas_call(
        functools.partial(_mm_body, n_a=n_a, nk=nk),
        out_shape=jax.ShapeDtypeStruct((m, n), out_dtype),
        grid=grid,
        in_specs=in_specs,
        out_specs=out_spec,
        scratch_shapes=scratch,
        compiler_params=_params(sem),
        name=name,
    )(*a_list, *b_list)


def _glu_body(a_ref, wg_ref, wu_ref, o_ref):
    a = a_ref[...]
    g = _dot(a, wg_ref[...])
    u = _dot(a, wu_ref[...])
    o_ref[...] = (g * _sigmoid(g) * u).astype(o_ref.dtype)


def _glu(a, wg, wu, m, tm, tn):
    n_e, kdim, n = wg.shape
    nj = n // tn
    w_spec = pl.BlockSpec((None, kdim, tn), lambda i, j: (j // nj, 0, j % nj))
    return pl.pallas_call(
        _glu_body,
        out_shape=jax.ShapeDtypeStruct((m, n_e * n), BF16),
        grid=(m // tm, n_e * nj),
        in_specs=[pl.BlockSpec((tm, kdim), lambda i, j: (i, 0)), w_spec, w_spec],
        out_specs=pl.BlockSpec((tm, tn), lambda i, j: (i, j)),
        compiler_params=_params(("parallel", "parallel")),
        name="glu",
    )(a, wg, wu)


def _router_body(u_ref, w_ref, o_ref, sel_ref, *, n_experts):
    lo, hi = _unpack_halves(u_ref[...])
    half = lo.shape[1]
    logits = _dot(lo, w_ref[pl.ds(0, half), :]) + _dot(hi, w_ref[pl.ds(half, half), :])
    lane = lax.broadcasted_iota(jnp.int32, logits.shape, 1)
    l1 = jnp.where(lane < n_experts, logits, NEG)
    m1 = jnp.max(l1, axis=-1, keepdims=True)
    i1 = jnp.min(jnp.where(l1 == m1, lane, LANES), axis=-1, keepdims=True)
    l2 = jnp.where(lane == i1, NEG, l1)
    m2 = jnp.max(l2, axis=-1, keepdims=True)
    i2 = jnp.min(jnp.where(l2 == m2, lane, LANES), axis=-1, keepdims=True)
    e2 = jnp.exp(m2 - m1)
    w1 = 1.0 / (1.0 + e2)
    w2 = e2 / (1.0 + e2)
    is1, is2 = lane == i1, lane == i2
    o_ref[...] = jnp.where(is1, w1, 0.0) + jnp.where(is2, w2, 0.0)
    sel_ref[...] = jnp.where(is1, 1.0, jnp.where(is2, 1.0, 0.0)).astype(sel_ref.dtype)


def _router(u_packed, w_router_padded, m, n_experts):
    half = u_packed.shape[1]
    tm = _tile(m, 1024, ROW_TILE)
    out = pl.BlockSpec((tm, LANES), lambda i: (i, 0))
    return pl.pallas_call(
        functools.partial(_router_body, n_experts=n_experts),
        out_shape=[jax.ShapeDtypeStruct((m, LANES), F32), jax.ShapeDtypeStruct((m, LANES), BF16)],
        grid=(m // tm,),
        in_specs=[pl.BlockSpec((tm, half), lambda i: (i, 0)), pl.BlockSpec((2 * half, LANES), lambda i: (0, 0))],
        out_specs=[out, out],
        compiler_params=_params(("parallel",)),
        name="router",
    )(u_packed, w_router_padded)


def _rank_body(sel_ref, rank_ref, cnt_ref, carry_ref):
    @pl.when(pl.program_id(0) == 0)
    def _():
        carry_ref[...] = jnp.zeros_like(carry_ref)

    sel = sel_ref[...]
    n = sel.shape[0]
    t = lax.broadcasted_iota(jnp.int32, (n, n), 0)
    s = lax.broadcasted_iota(jnp.int32, (n, n), 1)
    before = jnp.where(s < t, 1.0, 0.0).astype(BF16)
    carry = carry_ref[...]
    rank_ref[...] = _dot(before, sel) + carry[0:1, :]
    carry = carry + jnp.sum(sel.astype(F32), axis=0, keepdims=True)
    carry_ref[...] = carry
    cnt_ref[...] = carry


def _rank(sel):
    m = sel.shape[0]
    tile = _tile(m, RANK_TILE, ROW_TILE)
    blk = pl.BlockSpec((tile, LANES), lambda i: (i, 0))
    return pl.pallas_call(
        _rank_body,
        out_shape=[jax.ShapeDtypeStruct((m, LANES), F32), jax.ShapeDtypeStruct((8, LANES), F32)],
        grid=(m // tile,),
        in_specs=[blk],
        out_specs=[blk, pl.BlockSpec((8, LANES), lambda i: (0, 0))],
        scratch_shapes=[pltpu.VMEM((8, LANES), F32)],
        compiler_params=_params(("arbitrary",)),
        name="moe_rank",
    )(sel)


def _slots_body(rank_ref, sel_ref, comb_ref, cnt_ref, pos_ref, wts_ref):
    cnt = cnt_ref[...]
    n_tiles = jnp.floor((cnt + (MOE_TILE - 1)) * (1.0 / MOE_TILE))
    a = lax.broadcasted_iota(jnp.int32, (LANES, LANES), 0)
    b = lax.broadcasted_iota(jnp.int32, (LANES, LANES), 1)
    earlier = jnp.where(a < b, 1.0, 0.0).astype(BF16)
    start = _dot(n_tiles.astype(BF16), earlier)[0:1, :] * MOE_TILE
    sel = sel_ref[...].astype(F32) > 0.5
    lane = lax.broadcasted_iota(jnp.int32, sel.shape, 1)
    lane_a = jnp.min(jnp.where(sel, lane, LANES), axis=-1, keepdims=True)
    lane_b = jnp.max(jnp.where(sel, lane, -1), axis=-1, keepdims=True)
    row = start + rank_ref[...]
    comb = comb_ref[...]

    def pick(x, which):
        return jnp.sum(jnp.where(lane == which, x, 0.0), axis=-1, keepdims=True)

    pos_ref[...] = jnp.where(lane == 0, pick(row, lane_a), jnp.where(lane == 1, pick(row, lane_b), 0.0)).astype(jnp.int32)
    wts_ref[...] = jnp.where(lane == 0, pick(comb, lane_a), jnp.where(lane == 1, pick(comb, lane_b), 0.0))


def _slots(rank, sel, comb, counts):
    m = rank.shape[0]
    tm = _tile(m, 1024, ROW_TILE)
    blk = pl.BlockSpec((tm, LANES), lambda i: (i, 0))
    return pl.pallas_call(
        _slots_body,
        out_shape=[jax.ShapeDtypeStruct((m, LANES), jnp.int32), jax.ShapeDtypeStruct((m, LANES), F32)],
        grid=(m // tm,),
        in_specs=[blk, blk, blk, pl.BlockSpec((8, LANES), lambda i: (0, 0))],
        out_specs=[blk, blk],
        compiler_params=_params(("parallel",)),
        name="moe_slots",
    )(rank, sel, comb, counts)


def _row_copy(src_ref, src_row, dst_ref, dst_row, sem):
    return pltpu.make_async_copy(src_ref.at[pl.ds(src_row, 1)], dst_ref.at[pl.ds(dst_row, 1)], sem)


def _invert_body(pa_ref, pb_ref, src_ref, *, n_tokens):
    def clear(r, carry):
        src_ref[r] = 0
        return carry

    lax.fori_loop(0, src_ref.shape[0], clear, 0, unroll=SCALAR_UNROLL)

    def scatter(t, carry):
        src_ref[pa_ref[t]] = t
        src_ref[pb_ref[t]] = t
        return carry

    lax.fori_loop(0, n_tokens, scatter, 0, unroll=SCALAR_UNROLL)


def _invert(pos_a, pos_b, n_sorted):
    grid_spec = pltpu.PrefetchScalarGridSpec(
        num_scalar_prefetch=2, grid=(1,), in_specs=[],
        out_specs=pl.BlockSpec(memory_space=pltpu.SMEM))
    return pl.pallas_call(
        functools.partial(_invert_body, n_tokens=pos_a.shape[0]),
        out_shape=jax.ShapeDtypeStruct((n_sorted,), jnp.int32),
        grid_spec=grid_spec,
        compiler_params=_params(("arbitrary",)),
        name="moe_invert",
    )(pos_a, pos_b)


def _dispatch_body(src_ref, u_ref, xs_ref, sem):
    i = pl.program_id(0)
    tile = xs_ref.shape[0]

    def start(r, carry):
        _row_copy(u_ref, src_ref[i * tile + r], xs_ref, r, sem).start()
        return carry

    lax.fori_loop(0, tile, start, 0, unroll=SCALAR_UNROLL)

    def wait(r, carry):
        _row_copy(u_ref, 0, xs_ref, 0, sem).wait()
        return carry

    lax.fori_loop(0, tile, wait, 0, unroll=SCALAR_UNROLL)


def _dispatch(src, u_packed):
    n_sorted = src.shape[0]
    half = u_packed.shape[1]
    grid_spec = pltpu.PrefetchScalarGridSpec(
        num_scalar_prefetch=1, grid=(n_sorted // MOE_TILE,),
        in_specs=[pl.BlockSpec(memory_space=pl.ANY)],
        out_specs=pl.BlockSpec((MOE_TILE, half), lambda i, src: (i, 0)),
        scratch_shapes=[pltpu.SemaphoreType.DMA(())])
    return pl.pallas_call(
        _dispatch_body,
        out_shape=jax.ShapeDtypeStruct((n_sorted, half), U32),
        grid_spec=grid_spec,
        compiler_params=_params(("arbitrary",)),
        name="moe_dispatch",
    )(src, u_packed)


def _expert_glu_body(te_ref, x_ref, wg_ref, wu_ref, o_ref):
    del te_ref
    lo, hi = _unpack_halves(x_ref[...])
    half = lo.shape[1]
    g = _dot(lo, wg_ref[pl.ds(0, half), :]) + _dot(hi, wg_ref[pl.ds(half, half), :])
    u = _dot(lo, wu_ref[pl.ds(0, half), :]) + _dot(hi, wu_ref[pl.ds(half, half), :])
    o_ref[...] = (g * _sigmoid(g) * u).astype(o_ref.dtype)


def _expert_glu(tile_expert, xs, wg, wu, tn):
    n_sorted, half = xs.shape
    _, kdim, n = wg.shape
    w_spec = pl.BlockSpec((None, kdim, tn), lambda j, i, te: (te[i], 0, j))
    grid_spec = pltpu.PrefetchScalarGridSpec(
        num_scalar_prefetch=1, grid=(n // tn, n_sorted // MOE_TILE),
        in_specs=[pl.BlockSpec((MOE_TILE, half), lambda j, i, te: (i, 0)), w_spec, w_spec],
        out_specs=pl.BlockSpec((MOE_TILE, tn), lambda j, i, te: (i, j)))
    return pl.pallas_call(
        _expert_glu_body,
        out_shape=jax.ShapeDtypeStruct((n_sorted, n), BF16),
        grid_spec=grid_spec,
        compiler_params=_params(("parallel", "arbitrary")),
        name="moe_glu",
    )(tile_expert, xs, wg, wu)


def _expert_down_body(te_ref, h_ref, w_ref, o_ref):
    del te_ref
    o_ref[...] = _dot(h_ref[...], w_ref[...])


def _expert_down(tile_expert, hs, wd, tn):
    n_sorted, kdim = hs.shape
    n = wd.shape[2]
    grid_spec = pltpu.PrefetchScalarGridSpec(
        num_scalar_prefetch=1, grid=(n // tn, n_sorted // MOE_TILE),
        in_specs=[pl.BlockSpec((MOE_TILE, kdim), lambda j, i, te: (i, 0)),
                  pl.BlockSpec((None, kdim, tn), lambda j, i, te: (te[i], 0, j))],
        out_specs=pl.BlockSpec((MOE_TILE, tn), lambda j, i, te: (i, j)))
    return pl.pallas_call(
        _expert_down_body,
        out_shape=jax.ShapeDtypeStruct((n_sorted, n), F32),
        grid_spec=grid_spec,
        compiler_params=_params(("parallel", "arbitrary")),
        name="moe_down",
    )(tile_expert, hs, wd)


def _combine_post_body(pa_ref, pb_ref, os_ref, h_ref, wts_ref, gp_ref, gate_ref, ho_ref, buf, sems):
    i = pl.program_id(0)
    n = pl.num_programs(0)
    tile = h_ref.shape[0]
    slot = i % 2

    def issue(tile_idx, dst_slot):
        def one(t, carry):
            tok = tile_idx * tile + t
            _row_copy(os_ref, pa_ref[tok], buf.at[dst_slot, 0], t, sems.at[dst_slot]).start()
            _row_copy(os_ref, pb_ref[tok], buf.at[dst_slot, 1], t, sems.at[dst_slot]).start()
            return carry
        lax.fori_loop(0, tile, one, 0, unroll=SCALAR_UNROLL)

    @pl.when(i == 0)
    def _():
        issue(0, 0)

    @pl.when(i + 1 < n)
    def _():
        issue(i + 1, 1 - slot)

    def wait_one(t, carry):
        _row_copy(os_ref, 0, buf.at[slot, 0], 0, sems.at[slot]).wait()
        return carry

    lax.fori_loop(0, 2 * tile, wait_one, 0, unroll=SCALAR_UNROLL)
    wts = wts_ref[...]
    y = wts[:, 0:1] * buf[slot, 0] + wts[:, 1:2] * buf[slot, 1]
    ho_ref[...] = h_ref[...] + gate_ref[...] * _rms(y, gp_ref[...])


def _combine_post(pos_a, pos_b, os, h, wts, g_post, gate, group, n_rows):
    d = h.shape[1]
    row = lambda i, pa, pb: (i, 0)
    grid_spec = pltpu.PrefetchScalarGridSpec(
        num_scalar_prefetch=2, grid=(n_rows // ROW_TILE,),
        in_specs=[pl.BlockSpec(memory_space=pl.ANY),
                  pl.BlockSpec((ROW_TILE, d), row),
                  pl.BlockSpec((ROW_TILE, LANES), row),
                  pl.BlockSpec((1, d), lambda i, pa, pb: (0, 0)),
                  pl.BlockSpec((None, 1, d), lambda i, pa, pb: (group(i), 0, 0))],
        out_specs=pl.BlockSpec((ROW_TILE, d), row),
        scratch_shapes=[pltpu.VMEM((2, 2, ROW_TILE, d), F32), pltpu.SemaphoreType.DMA((2,))])
    return pl.pallas_call(
        _combine_post_body,
        out_shape=jax.ShapeDtypeStruct((n_rows, d), F32),
        grid_spec=grid_spec,
        compiler_params=_params(("arbitrary",)),
        name="moe_combine_post",
    )(pos_a, pos_b, os, h, wts, g_post, gate)


def _split3(x):
    hi = x.astype(BF16)
    r1 = x - hi.astype(F32)
    mid = r1.astype(BF16)
    lo = (r1 - mid.astype(F32)).astype(BF16)
    return hi, mid, lo


def _gates_body(g_ref, b_ref, o_ref, *, n_heads):
    pre = g_ref[...] + b_ref[...]
    log_f = jnp.minimum(pre, 0.0) - jnp.log(1.0 + jnp.exp(-jnp.abs(pre)))
    lc = pre.shape[0]
    t = lax.broadcasted_iota(jnp.int32, (lc, lc), 0)
    s = lax.broadcasted_iota(jnp.int32, (lc, lc), 1)
    tri_f = jnp.where(s <= t, 1.0, 0.0).astype(BF16)
    tri_b = jnp.where(s >= t, 1.0, 0.0).astype(BF16)
    hi, mid, lo = _split3(log_f)
    cs_f = _dot(tri_f, hi) + _dot(tri_f, mid) + _dot(tri_f, lo)
    cs_b = _dot(tri_b, hi) + _dot(tri_b, mid) + _dot(tri_b, lo)
    lane = lax.broadcasted_iota(jnp.int32, pre.shape, 1)
    kind = lane // n_heads
    a_f = pre - pltpu.roll(cs_f, LANES - n_heads, 1)
    a_b = pre - pltpu.roll(cs_b, LANES - n_heads, 1)
    o_ref[...] = jnp.where(kind == 0, a_f, jnp.where(kind == 1, cs_f, jnp.where(kind == 2, a_b, cs_b)))


def _gates(g_raw, bias_row, n_heads):
    t_all = g_raw.shape[0]
    blk = pl.BlockSpec((MLSTM_CHUNK, LANES), lambda i: (i, 0))
    return pl.pallas_call(
        functools.partial(_gates_body, n_heads=n_heads),
        out_shape=jax.ShapeDtypeStruct((t_all, LANES), F32),
        grid=(t_all // MLSTM_CHUNK,),
        in_specs=[blk, pl.BlockSpec((1, LANES), lambda i: (0, 0))],
        out_specs=blk,
        compiler_params=_params(("parallel",)),
        name="mlstm_gates",
    )(g_raw, bias_row)


STATE_ROWS = HEAD_DIM + 16


def _mlstm_chunk(q, k, vte, a_col, a_row, b_row, state, rev):
    ct, m_st = state
    lc = q.shape[0]
    b_last = b_row[:, 0:1] if rev else b_row[:, lc - 1:lc]
    m_new = jnp.maximum(b_last + m_st, b_last + jnp.max(a_row, axis=-1, keepdims=True))
    decay = jnp.exp(b_last + m_st - m_new)
    w_row = jnp.exp(b_last + a_row - m_new)
    lhs = (vte.astype(F32) * w_row).astype(BF16)
    ct_new = decay * ct + _dot(lhs, k)

    j = lax.broadcasted_iota(jnp.int32, (lc, lc), 0)
    t = lax.broadcasted_iota(jnp.int32, (lc, lc), 1)
    seen = (j >= t) if rev else (j <= t)
    dmat = jnp.where(seen, a_col + b_row, NEG)
    m_inter = b_row + m_st
    m_t = jnp.maximum(m_inter, jnp.max(dmat, axis=0, keepdims=True))
    s = _dot_nt(k, q) * jnp.exp(dmat - m_t)
    inter = jnp.exp(m_inter - m_t)
    from_state = _dot_nt(ct.astype(BF16), q)
    num = _dot(vte, s.astype(BF16))[:HEAD_DIM] + inter * from_state[:HEAD_DIM]
    den = jnp.sum(s, axis=0, keepdims=True) + inter * from_state[HEAD_DIM:HEAD_DIM + 1]
    h = num / jnp.maximum(jnp.abs(den), jnp.exp(-m_t))
    return h, (ct_new, m_new)


def _mlstm_body(qc_ref, kc_ref, vtc_ref, oc_ref, ql_ref, kl_ref, vtl_ref, ol_ref,
                acf_ref, acb_ref, grc_ref, alf_ref, alb_ref, grl_ref, cos_ref, sin_ref, ng_ref, *rest, with_ctx_out):
    if with_ctx_out:
        yl_ref, yc_ref, qs_ref, ks_ref, vte_ref, hf_ref, hb_ref = rest
    else:
        yl_ref, qs_ref, ks_ref, vte_ref, hf_ref, hb_ref = rest
    lc = MLSTM_CHUNK
    n_lat = ql_ref.shape[0] // lc
    n_ctx = qc_ref.shape[0] // lc
    scale = HEAD_DIM ** -0.5
    pad_row = lax.broadcasted_iota(jnp.int32, (STATE_ROWS - HEAD_DIM, lc), 0)
    ones_pad = jnp.where(pad_row == 0, 1.0, 0.0).astype(BF16)
    quarter = HEAD_DIM // 4
    src = lax.broadcasted_iota(jnp.int32, (HEAD_DIM, HEAD_DIM), 0)
    dst = lax.broadcasted_iota(jnp.int32, (HEAD_DIM, HEAD_DIM), 1)
    rot = jnp.where((dst % (2 * quarter)) < quarter,
                    jnp.where(src == dst + quarter, -1.0, 0.0),
                    jnp.where(src == dst - quarter, 1.0, 0.0)).astype(BF16)

    def rope(x, rows):
        return x.astype(F32) * cos_ref[rows, :] + _dot(x, rot) * sin_ref[rows, :]

    def prep_chunk(i, carry):
        rows = pl.ds(pl.multiple_of(i * lc, lc), lc)
        qs_ref[rows, :] = (rope(ql_ref[rows, :], rows) * scale).astype(BF16)
        ks_ref[rows, :] = rope(kl_ref[rows, :], rows).astype(BF16)
        vte_ref[pl.ds(0, HEAD_DIM), rows] = vtl_ref[:, rows]
        vte_ref[pl.ds(HEAD_DIM, STATE_ROWS - HEAD_DIM), rows] = ones_pad
        return carry

    lax.fori_loop(0, n_lat, prep_chunk, 0)

    ng = ng_ref[...]

    def finish(h_t, o_gate):
        y_t = h_t * lax.rsqrt(jnp.mean(h_t * h_t, axis=0, keepdims=True) + EPS)
        return (_sigmoid(o_gate.astype(F32)) * (y_t.T * ng)).astype(BF16)

    zero = (jnp.zeros((STATE_ROWS, HEAD_DIM), F32), jnp.zeros((1, 1), F32))
    a_cols_c, a_cols_l = (acf_ref, acb_ref), (alf_ref, alb_ref)

    st = [zero, zero]
    for i in range(n_ctx):
        for d in range(2):
            ci = i if d == 0 else n_ctx - 1 - i
            rows = pl.ds(ci * lc, lc)
            q = (qc_ref[rows, :].astype(F32) * scale).astype(BF16)
            vte = jnp.concatenate([vtc_ref[:, rows], ones_pad], axis=0)
            h, st[d] = _mlstm_chunk(q, kc_ref[rows, :], vte, a_cols_c[d][rows, :], grc_ref[pl.ds(2 * d, 1), rows],
                                    grc_ref[pl.ds(2 * d + 1, 1), rows], st[d], d == 1)
            if with_ctx_out:
                (hf_ref if d == 0 else hb_ref)[:, rows] = h
    if with_ctx_out:
        for i in range(n_ctx):
            rows = pl.ds(i * lc, lc)
            yc_ref[rows, :] = finish(hf_ref[:, rows] + hb_ref[:, rows], oc_ref[rows, :])

    def lat_step(i, carry):
        out = []
        for d, st_d in enumerate(carry):
            ci = i if d == 0 else n_lat - 1 - i
            rows = pl.ds(pl.multiple_of(ci * lc, lc), lc)
            h, st_d = _mlstm_chunk(qs_ref[rows, :], ks_ref[rows, :], vte_ref[:, rows], a_cols_l[d][rows, :],
                                   grl_ref[pl.ds(2 * d, 1), rows], grl_ref[pl.ds(2 * d + 1, 1), rows], st_d, d == 1)
            (hf_ref if d == 0 else hb_ref)[:, rows] = h
            out.append(st_d)
        return tuple(out)

    lax.fori_loop(0, n_lat, lat_step, (st[0], st[1]))

    def finish_chunk(i, carry):
        rows = pl.ds(pl.multiple_of(i * lc, lc), lc)
        yl_ref[rows, :] = finish(hf_ref[:, rows] + hb_ref[:, rows], ol_ref[rows, :])
        return carry

    lax.fori_loop(0, n_lat, finish_chunk, 0)


def _mlstm(p, v_t, a_col, g_row, cos, sin, norm_g, n_batch, seq, ctx_len, n_heads, with_ctx_out):
    ctx_blk0 = n_batch * seq // ctx_len

    def col(kind, lat):
        if lat:
            return pl.BlockSpec((seq, HEAD_DIM), lambda b, h: (b, kind * n_heads + h))
        return pl.BlockSpec((ctx_len, HEAD_DIM), lambda b, h: (ctx_blk0 + b, kind * n_heads + h))

    def a_spec(d, lat):
        if lat:
            return pl.BlockSpec((None, None, seq, 1), lambda b, h: (h, d, b, 0))
        return pl.BlockSpec((None, None, ctx_len, 1), lambda b, h: (h, d, ctx_blk0 + b, 0))

    in_specs = [
        col(0, False), col(1, False), pl.BlockSpec((HEAD_DIM, ctx_len), lambda b, h: (h, ctx_blk0 + b)), col(3, False),
        col(0, True), col(1, True), pl.BlockSpec((HEAD_DIM, seq), lambda b, h: (h, b)), col(3, True),
        a_spec(0, False), a_spec(1, False), pl.BlockSpec((None, 4, ctx_len), lambda b, h: (h, 0, ctx_blk0 + b)),
        a_spec(0, True), a_spec(1, True), pl.BlockSpec((None, 4, seq), lambda b, h: (h, 0, b)),
        pl.BlockSpec((seq, HEAD_DIM), lambda b, h: (0, 0)),
        pl.BlockSpec((seq, HEAD_DIM), lambda b, h: (0, 0)),
        pl.BlockSpec((None, 1, HEAD_DIM), lambda b, h: (h, 0, 0)),
    ]
    out_shape = [jax.ShapeDtypeStruct((n_batch * seq, n_heads * HEAD_DIM), BF16)]
    out_specs = [pl.BlockSpec((seq, HEAD_DIM), lambda b, h: (b, h))]
    if with_ctx_out:
        out_shape.append(jax.ShapeDtypeStruct((n_batch * ctx_len, n_heads * HEAD_DIM), BF16))
        out_specs.append(pl.BlockSpec((ctx_len, HEAD_DIM), lambda b, h: (b, h)))
    return pl.pallas_call(
        functools.partial(_mlstm_body, with_ctx_out=with_ctx_out),
        out_shape=out_shape,
        grid=(n_batch, n_heads),
        in_specs=in_specs,
        out_specs=out_specs,
        scratch_shapes=[pltpu.VMEM((seq, HEAD_DIM), BF16), pltpu.VMEM((seq, HEAD_DIM), BF16),
                        pltpu.VMEM((STATE_ROWS, seq), BF16),
                        pltpu.VMEM((HEAD_DIM, seq), F32), pltpu.VMEM((HEAD_DIM, seq), F32)],
        compiler_params=_params(("parallel", "parallel")),
        name="mlstm",
    )(p, p, v_t, p, p, p, v_t, p, a_col, a_col, g_row, a_col, a_col, g_row, cos, sin, norm_g)


def _conv_body(ap_ref, gp_ref, a_ref, g_ref, an_ref, gn_ref, w_ref, b_ref, ng_ref, o_ref, u_ref, c_ref, s_ref,
               *, n_lat_tiles, lat_tiles_per_seq, width):
    i = pl.program_id(0)
    is_lat = i < n_lat_tiles
    pos = i % lat_tiles_per_seq
    first = jnp.logical_or(jnp.logical_not(is_lat), pos == 0)
    last = jnp.logical_or(jnp.logical_not(is_lat), pos == lat_tiles_per_seq - 1)

    def glu(x_ref, y_ref):
        return x_ref[...].astype(F32) * _sigmoid(y_ref[...].astype(F32))

    tile = a_ref.shape[0]
    u_ref[pl.ds(0, HALO), :] = jnp.where(first, 0.0, 1.0) * glu(ap_ref, gp_ref)
    u_ref[pl.ds(HALO, tile), :] = glu(a_ref, g_ref)
    u_ref[pl.ds(HALO + tile, HALO), :] = jnp.where(last, 0.0, 1.0) * glu(an_ref, gn_ref)

    strip, cblk = 32, 256
    n_ch = a_ref.shape[1]
    cblk = min(cblk, n_ch)
    base = HALO - width // 2
    span = s_ref.shape[1]
    for phase in range(1, SUBLANES):
        s_ref[phase] = u_ref[pl.ds(phase, span), :]

    for r0 in range(0, tile, strip):
        for cb in range(n_ch // cblk):
            cols = pl.ds(cb * cblk, cblk)
            acc = jnp.zeros((strip, cblk), F32)
            for k in range(width):
                phase, row = (base + k) % SUBLANES, r0 + (base + k) // SUBLANES * SUBLANES
                tap = u_ref[pl.ds(row, strip), cols] if phase == 0 else s_ref[phase, pl.ds(row, strip), cols]
                acc += w_ref[pl.ds(k, 1), cols] * tap
            c_ref[pl.ds(r0, strip), cols] = acc
    y = _rms(c_ref[...] + b_ref[...], ng_ref[...])
    o_ref[...] = (y * _sigmoid(y)).astype(o_ref.dtype)


def _conv(p, conv_w, conv_b, conv_norm, a_col_blk, n_rows, n_lat_tiles, lat_tiles_per_seq):
    width, n_ch = conv_w.shape
    assert width // 2 <= HALO
    per = ROW_TILE // HALO
    last_halo = p.shape[0] // HALO - 1

    def cur(c):
        return pl.BlockSpec((ROW_TILE, n_ch), lambda i: (i, c))

    def prev(c):
        return pl.BlockSpec((HALO, n_ch), lambda i: (jnp.maximum(i * per - 1, 0), c))

    def nxt(c):
        return pl.BlockSpec((HALO, n_ch), lambda i: (jnp.minimum((i + 1) * per, last_halo), c))

    vec = pl.BlockSpec((1, n_ch), lambda i: (0, 0))
    a, g = a_col_blk, a_col_blk + 1
    return pl.pallas_call(
        functools.partial(_conv_body, n_lat_tiles=n_lat_tiles, lat_tiles_per_seq=lat_tiles_per_seq, width=width),
        out_shape=jax.ShapeDtypeStruct((n_rows, n_ch), BF16),
        grid=(n_rows // ROW_TILE,),
        in_specs=[prev(a), prev(g), cur(a), cur(g), nxt(a), nxt(g),
                  pl.BlockSpec((width, n_ch), lambda i: (0, 0)), vec, vec],
        out_specs=pl.BlockSpec((ROW_TILE, n_ch), lambda i: (i, 0)),
        scratch_shapes=[pltpu.VMEM((ROW_TILE + 2 * HALO, n_ch), F32), pltpu.VMEM((ROW_TILE, n_ch), F32),
                        pltpu.VMEM((SUBLANES, ROW_TILE + 2 * HALO - SUBLANES, n_ch), F32)],
        compiler_params=_params(("parallel",)),
        name="conformer_conv",
    )(p, p, p, p, p, p, conv_w, conv_b, conv_norm)


def _na_plan(rows):
    assert rows % NA_GROUP == 0 and rows >= NA_UNION
    plans = []
    for g in range(rows // NA_GROUP):
        union_start = min(max(g * NA_GROUP - NA_ROWS // 2, 0), rows - NA_UNION)
        plan = []
        for r in range(g * NA_GROUP, (g + 1) * NA_GROUP):
            row_start = min(max(r - NA_ROWS // 2, 0), rows - NA_ROWS)
            offset = row_start - union_start
            first_tile = row_start - r + NA_ROWS - 1
            assert 0 <= offset <= NA_UNION - NA_ROWS and first_tile - offset + NA_PAD_TILES >= 0
            plan.append((first_tile - offset + NA_PAD_TILES, offset))
        plans.append(tuple(plan))
    assert all(p == plans[1] for p in plans[1:-1])
    return plans[0], plans[1], plans[-1]


def _na_body(*refs, rows, with_ctx_out):
    if with_ctx_out:
        q_ref, k_ref, v_ref, kc_ref, vc_ref, bias_ref, qc_ref, o_ref, oc_ref = refs
    else:
        q_ref, k_ref, v_ref, kc_ref, vc_ref, bias_ref, o_ref = refs
    scale = HEAD_DIM ** -0.5
    kc = kc_ref[...]
    vc = vc_ref[...]
    n_groups = rows // NA_GROUP

    def row_group(g, carry):
        rows_g = pl.ds(pl.multiple_of(g * (NA_GROUP * GRID_W), NA_GROUP * GRID_W), NA_GROUP * GRID_W)
        union_start = jnp.clip(g * NA_GROUP - NA_ROWS // 2, 0, rows - NA_UNION)
        win = pl.ds(pl.multiple_of(union_start * GRID_W, GRID_W), NA_UNION * GRID_W)
        variant = jnp.where(g == 0, 0, jnp.where(g == n_groups - 1, 2, 1))
        q_g = q_ref[rows_g, :]
        s_win = _dot_nt(q_g, k_ref[win, :]) * scale + bias_ref[variant]
        s_ctx = _dot_nt(q_g, kc) * scale
        m = jnp.maximum(jnp.max(s_win, axis=-1, keepdims=True), jnp.max(s_ctx, axis=-1, keepdims=True))
        p_win = jnp.exp(s_win - m)
        p_ctx = jnp.exp(s_ctx - m)
        denom = jnp.sum(p_win, axis=-1, keepdims=True) + jnp.sum(p_ctx, axis=-1, keepdims=True)
        out = (_dot(p_win.astype(BF16), v_ref[win, :]) + _dot(p_ctx.astype(BF16), vc)) / denom
        o_ref[rows_g, :] = out.astype(o_ref.dtype)
        return carry

    lax.fori_loop(0, n_groups, row_group, 0, unroll=2)

    if with_ctx_out:
        s = _dot_nt(qc_ref[...], kc) * scale
        p = jnp.exp(s - jnp.max(s, axis=-1, keepdims=True))
        oc_ref[...] = (_dot(p.astype(BF16), vc) / jnp.sum(p, axis=-1, keepdims=True)).astype(oc_ref.dtype)


def _na(p, bias, col0, n_batch, seq, ctx_len, n_heads, with_ctx_out):
    ctx_blk0 = n_batch * seq // ctx_len
    rows = seq // GRID_W

    def lat(kind):
        return pl.BlockSpec((seq, HEAD_DIM), lambda b, h: (b, col0 + kind * n_heads + h))

    def ctx(kind):
        return pl.BlockSpec((ctx_len, HEAD_DIM), lambda b, h: (ctx_blk0 + b, col0 + kind * n_heads + h))

    in_specs = [lat(0), lat(1), lat(2), ctx(1), ctx(2),
                pl.BlockSpec((None,) + bias.shape[1:], lambda b, h: (h, 0, 0, 0))]
    args = [p, p, p, p, p, bias]
    out_shape = [jax.ShapeDtypeStruct((n_batch * seq, n_heads * HEAD_DIM), BF16)]
    out_specs = [pl.BlockSpec((seq, HEAD_DIM), lambda b, h: (b, h))]
    if with_ctx_out:
        in_specs.append(ctx(0))
        args.append(p)
        out_shape.append(jax.ShapeDtypeStruct((n_batch * ctx_len, n_heads * HEAD_DIM), BF16))
        out_specs.append(pl.BlockSpec((ctx_len, HEAD_DIM), lambda b, h: (b, h)))
    return pl.pallas_call(
        functools.partial(_na_body, rows=rows, with_ctx_out=with_ctx_out),
        out_shape=out_shape,
        grid=(n_batch, n_heads),
        in_specs=in_specs,
        out_specs=out_specs,
        compiler_params=_params(("parallel", "parallel")),
        name="neighbourhood_attention",
    )(*args)


def _na_bias_body(r_ref, o_ref, *, plans):
    n_taps = 2 * NA_COLS - 1
    width = r_ref.shape[1]
    q = lax.broadcasted_iota(jnp.int32, (GRID_W, width), 0)
    k = lax.broadcasted_iota(jnp.int32, (GRID_W, width), 1) % GRID_W
    tap = k - q + (NA_COLS - 1)
    strip = jnp.zeros((GRID_W, width), F32)
    for j in range(n_taps):
        strip = jnp.where(tap == j, r_ref[pl.ds(j, 1), :], strip)
    col_start = jnp.clip(q - NA_COLS // 2, 0, GRID_W - NA_COLS)
    strip = jnp.where(k >= col_start, jnp.where(k < col_start + NA_COLS, strip, NEG), NEG)
    win = NA_UNION * GRID_W
    key_row = lax.broadcasted_iota(jnp.int32, (GRID_W, win), 1) // GRID_W
    for v, plan in enumerate(plans):
        for i, (shift, offset) in enumerate(plan):
            piece = strip[:, shift * GRID_W:shift * GRID_W + win]
            piece = jnp.where(key_row >= offset, jnp.where(key_row < offset + NA_ROWS, piece, NEG), NEG)
            o_ref[v, pl.ds(i * GRID_W, GRID_W), :] = piece


def _na_bias_table(rpb, rows):
    n_heads, n_dr, n_taps = rpb.shape
    assert n_dr == 2 * NA_ROWS - 1 and n_taps == 2 * NA_COLS - 1
    plans = _na_plan(rows)
    n_tiles = max(shift for plan in plans for shift, _ in plan) + NA_UNION
    width = -(-(n_tiles * GRID_W) // LANES) * LANES
    spread = jnp.repeat(rpb.transpose(0, 2, 1), GRID_W, axis=2)
    lead = NA_PAD_TILES * GRID_W
    spread = jnp.pad(spread, ((0, 0), (0, 32 - n_taps), (lead, width - lead - n_dr * GRID_W)))
    out_blk = (len(plans), NA_GROUP * GRID_W, NA_UNION * GRID_W)
    return pl.pallas_call(
        functools.partial(_na_bias_body, plans=plans),
        out_shape=jax.ShapeDtypeStruct((n_heads,) + out_blk, F32),
        grid=(n_heads,),
        in_specs=[pl.BlockSpec((None, 32, width), lambda h: (h, 0, 0))],
        out_specs=pl.BlockSpec((None,) + out_blk, lambda h: (h, 0, 0, 0)),
        compiler_params=_params(("parallel",)),
        name="na_bias",
    )(spread)


def _rope_tables(length):
    t = jnp.arange(length)
    n_freq = HEAD_DIM // 4
    freqs = ROPE_BASE ** (-jnp.arange(n_freq, dtype=F32) / n_freq)
    ang_r = (t // GRID_W).astype(F32)[:, None] * freqs
    ang_c = (t % GRID_W).astype(F32)[:, None] * freqs
    ang = jnp.concatenate([ang_r, ang_r, ang_c, ang_c], axis=-1)
    return jnp.cos(ang), jnp.sin(ang)


def kernel(x, c, ctx, c_ctx, w_mod, b_mod, g_norm, w_in, w_out, mlstm_gate_bias, mlstm_norm, conv_w, conv_b,
           conv_norm, na_rpb, ffn_w_gate, ffn_w_up, ffn_w_down, router_w, moe_w_gate, moe_w_up, moe_w_down):
    n_batch, seq, d = x.shape
    ctx_len = ctx.shape[1]
    depth = w_mod.shape[0]
    m_width = mlstm_norm.shape[1]
    c_width = conv_w.shape[2]
    n_width = d - m_width - c_width
    mh, nh = m_width // HEAD_DIM, n_width // HEAD_DIM
    n_experts = router_w.shape[2]
    t_lat, t_ctx = n_batch * seq, n_batch * ctx_len
    t_all = t_lat + t_ctx
    assert seq % MLSTM_CHUNK == 0 and ctx_len % MLSTM_CHUNK == 0 and ctx_len == ROW_TILE
    assert t_lat % ctx_len == 0 and (4 * m_width) % c_width == 0 and 4 * mh <= LANES
    n_lat_tiles, lat_tiles_per_seq = t_lat // ROW_TILE, seq // ROW_TILE
    group = _group_map(n_lat_tiles, lat_tiles_per_seq, n_batch)

    h = jnp.concatenate([x.reshape(t_lat, d), ctx.reshape(t_ctx, d)], axis=0)
    cond = jnp.concatenate([c, jnp.broadcast_to(c_ctx[None], (8 - n_batch, d))], axis=0)
    mod = _modulation(cond, w_mod, b_mod)
    grp_rows = jnp.concatenate([jnp.arange(n_batch), jnp.full((n_batch,), n_batch)])
    cos, sin = _rope_tables(seq)
    tm_all, tm_lat = _tile(t_all, 1024, ROW_TILE), _tile(t_lat, 1024, ROW_TILE)

    def mods(layer):
        m6 = mod[layer][grp_rows].reshape(2 * n_batch, N_MOD, 1, d)
        return [m6[:, k] for k in range(N_MOD)]

    gv = lambda layer, k: g_norm[layer, k].reshape(1, d)

    sh1, sc1, _, _, _, _ = mods(0)
    u = _normmod(h, gv(0, 0), sh1, sc1, group, t_all)

    for layer in range(depth):
        last = layer == depth - 1
        with_ctx_out = not last
        sh1, sc1, gt1, sh2, sc2, gt2 = mods(layer)
        rows_out = t_lat if last else t_all
        tm_out = tm_lat if last else tm_all

        w = w_in[layer]
        cut_g, cut_c = 4 * m_width, 4 * m_width + 4 * mh
        w_main = jnp.concatenate([w[:, :cut_g], w[:, cut_c:]], axis=1).astype(BF16)
        w_gate = jnp.pad(w[:, cut_g:cut_c], ((0, 0), (0, LANES - 4 * mh))).astype(BF16)
        n_main = w_main.shape[1]
        p = _matmul([u], [w_main], BF16, t_all, tm_all, _tile(n_main, 1280, 2 * LANES), name="w_in")
        g_raw = _matmul([u], [w_gate], F32, t_all, tm_all, LANES, name="w_in_gates")

        bias_row = jnp.pad(mlstm_gate_bias[layer].reshape(1, 4 * mh), ((0, 0), (0, LANES - 4 * mh)))
        g_tab = _gates(g_raw, bias_row, mh)[:, :4 * mh].reshape(t_all, 4, mh)
        g_row = g_tab.transpose(2, 1, 0)
        a_col = g_row[:, 0::2, :, None]
        v_t = p[:, 2 * m_width:3 * m_width].T
        ym = _mlstm(p, v_t, a_col, g_row, cos, sin, mlstm_norm[layer].reshape(mh, 1, HEAD_DIM),
                    n_batch, seq, ctx_len, mh, with_ctx_out)
        yc = _conv(p, conv_w[layer], conv_b[layer].reshape(1, c_width), conv_norm[layer].reshape(1, c_width),
                   cut_g // c_width, rows_out, n_lat_tiles, lat_tiles_per_seq)
        bias = _na_bias_table(na_rpb[layer], seq // GRID_W)
        yn = _na(p, bias, (cut_g + 2 * c_width) // HEAD_DIM, n_batch, seq, ctx_len, nh, with_ctx_out)
        if with_ctx_out:
            ym, yn = jnp.concatenate(ym, axis=0), jnp.concatenate(yn, axis=0)
        else:
            ym, yn = ym[0], yn[0]

        wo = w_out[layer]
        wo_parts = [wo[:m_width].astype(BF16), wo[m_width:m_width + c_width].astype(BF16),
                    wo[m_width + c_width:].astype(BF16)]
        mix = _matmul([ym, yc, yn], wo_parts, F32, rows_out, tm_out, _tile(d, 1024, 2 * LANES), name="w_out")
        moe = layer % 2 == 1
        h, u2 = _post(h, mix, gv(layer, 1), gt1, group, rows_out, nxt=(gv(layer, 2), sh2, sc2), pack_next=moe)

        j = layer // 2
        if not moe:
            d_ff = ffn_w_gate.shape[2]
            hid = _glu(u2, ffn_w_gate[j][None].astype(BF16), ffn_w_up[j][None].astype(BF16),
                       rows_out, tm_out, _tile(d_ff, 512, 2 * LANES))
            f = _matmul([hid], [ffn_w_down[j].astype(BF16)], F32, rows_out, tm_out, _tile(d, 1024, 2 * LANES),
                        tk=_tile(d_ff, 2048, 2 * LANES), name="ffn_down")
        else:
            d_e = moe_w_gate.shape[3]
            w_r = jnp.pad(router_w[j], ((0, 0), (0, LANES - n_experts))).astype(BF16)
            comb, sel = _router(u2, w_r, rows_out, n_experts)
            rank, counts = _rank(sel)
            pos, wts = _slots(rank, sel, comb, counts)
            pos_a, pos_b = pos[:, 0], pos[:, 1]
            n_sorted = TOP_K * rows_out + n_experts * MOE_TILE
            tiles_per_expert = (counts[0, :n_experts].astype(jnp.int32) + MOE_TILE - 1) // MOE_TILE
            tile_end = jnp.cumsum(tiles_per_expert)
            tile_expert = jnp.minimum(
                jnp.sum(jnp.arange(n_sorted // MOE_TILE)[:, None] >= tile_end[None, :], axis=1), n_experts - 1
            ).astype(jnp.int32)
            xs = _dispatch(_invert(pos_a, pos_b, n_sorted), u2)
            hs = _expert_glu(tile_expert, xs, moe_w_gate[j].astype(BF16), moe_w_up[j].astype(BF16),
                             _tile(d_e, 1024, 2 * LANES))
            os = _expert_down(tile_expert, hs, moe_w_down[j].astype(BF16), _tile(d, 1024, 2 * LANES))

        if moe:
            assert last
            h = _combine_post(pos_a, pos_b, os, h, wts, gv(layer, 3), gt2, group, rows_out)
        elif last:
            (h,) = _post(h, f, gv(layer, 3), gt2, group, rows_out)
        else:
            nsh1, nsc1 = mods(layer + 1)[:2]
            h, u = _post(h, f, gv(layer, 3), gt2, group, rows_out, nxt=(gv(layer + 1, 0), nsh1, nsc1))

    return h[:t_lat].reshape(n_batch, seq, d)
```

```python
import functools

import jax
import jax.numpy as jnp
from jax import lax
from jax.experimental import pallas as pl
from jax.experimental.pallas import tpu as pltpu

HEAD_DIM = 128
GRID_W = 64
NA_ROWS = 8
NA_COLS = 16
NA_GROUP = 4
NA_UNION = NA_ROWS + NA_GROUP - 1
NA_PAD_TILES = NA_UNION - NA_ROWS
ROPE_BASE = 10000.0
N_MOD = 6
TOP_K = 2
EPS = 1e-6
MLSTM_CHUNK = 256
ROW_TILE = 256
HALO = 16
LANES = 128
SUBLANES = 8
NEG = -1e30
VMEM_LIMIT = 56 * 1024 * 1024
MOE_TILE = 256
RANK_TILE = 512
SCALAR_UNROLL = 8

F32 = jnp.float32
BF16 = jnp.bfloat16
U32 = jnp.uint32


def _params(sem):
    return pltpu.CompilerParams(dimension_semantics=sem, vmem_limit_bytes=VMEM_LIMIT)


def _tile(n, target, mult=LANES):
    if n <= target:
        return n
    for step in (mult, LANES, 8):
        for t in range((target // step) * step, 0, -step):
            if n % t == 0:
                return t
    raise ValueError((n, target, mult))


def _dot(a, b):
    return jnp.dot(a, b, preferred_element_type=F32)


def _dot_nt(a, b):
    return lax.dot_general(a, b, (((1,), (1,)), ((), ())), preferred_element_type=F32)


def _sigmoid(x):
    return 1.0 / (1.0 + jnp.exp(-x))


def _rms(x, g):
    return x * lax.rsqrt(jnp.mean(x * x, axis=-1, keepdims=True) + EPS) * g


def _mod_body(c_ref, w_ref, b_ref, o_ref):
    k = pl.program_id(2)
    c = c_ref[...]
    s = (c * _sigmoid(c)).astype(BF16)
    part = _dot(s, w_ref[...].astype(BF16))

    @pl.when(k == 0)
    def _():
        o_ref[...] = part + b_ref[...]

    @pl.when(k > 0)
    def _():
        o_ref[...] += part


def _modulation(cond, w_mod, b_mod):
    depth, d, n = w_mod.shape
    tn, tk = _tile(n, 2048), _tile(d, 1024)
    return pl.pallas_call(
        _mod_body,
        out_shape=jax.ShapeDtypeStruct((depth, 8, n), F32),
        grid=(depth, n // tn, d // tk),
        in_specs=[pl.BlockSpec((8, tk), lambda l, j, k: (0, k)),
                  pl.BlockSpec((None, tk, tn), lambda l, j, k: (l, k, j)),
                  pl.BlockSpec((None, 1, tn), lambda l, j, k: (l, 0, j))],
        out_specs=pl.BlockSpec((None, 8, tn), lambda l, j, k: (l, 0, j)),
        compiler_params=_params(("parallel", "parallel", "arbitrary")),
        name="modulation",
    )(cond, w_mod, b_mod.reshape(depth, 1, n))


def _group_map(n_lat_tiles, lat_tiles_per_seq, n_batch):
    def group(i):
        return jnp.where(i < n_lat_tiles, i // lat_tiles_per_seq, n_batch + (i - n_lat_tiles))
    return group


def _normmod_body(h_ref, g_ref, sh_ref, sc_ref, u_ref):
    y = _rms(h_ref[...], g_ref[...])
    u_ref[...] = (y * (1.0 + sc_ref[...]) + sh_ref[...]).astype(u_ref.dtype)


def _normmod(h, g, sh, sc, group, n_rows):
    d = h.shape[1]
    row = pl.BlockSpec((ROW_TILE, d), lambda i: (i, 0))
    vec = pl.BlockSpec((1, d), lambda i: (0, 0))
    per_group = pl.BlockSpec((None, 1, d), lambda i: (group(i), 0, 0))
    return pl.pallas_call(
        _normmod_body,
        out_shape=jax.ShapeDtypeStruct((n_rows, d), BF16),
        grid=(n_rows // ROW_TILE,),
        in_specs=[row, vec, per_group, per_group],
        out_specs=row,
        compiler_params=_params(("parallel",)),
        name="normmod",
    )(h, g, sh, sc)


def _pack_halves(x):
    half = x.shape[1] // 2
    bits = lax.bitcast_convert_type(x.astype(BF16).astype(F32), U32)
    return (bits[:, half:] & jnp.uint32(0xFFFF0000)) | lax.shift_right_logical(bits[:, :half], jnp.uint32(16))


def _unpack_halves(w):
    lo = lax.bitcast_convert_type(lax.shift_left(w, jnp.uint32(16)), F32).astype(BF16)
    hi = lax.bitcast_convert_type(w & jnp.uint32(0xFFFF0000), F32).astype(BF16)
    return lo, hi


def _post_body(h_ref, y_ref, gp_ref, gate_ref, *rest, with_next, pack_next):
    h = h_ref[...] + gate_ref[...] * _rms(y_ref[...], gp_ref[...])
    if with_next:
        gn_ref, sh_ref, sc_ref, ho_ref, u_ref = rest
        ho_ref[...] = h
        u = _rms(h, gn_ref[...]) * (1.0 + sc_ref[...]) + sh_ref[...]
        u_ref[...] = _pack_halves(u) if pack_next else u.astype(u_ref.dtype)
    else:
        (ho_ref,) = rest
        ho_ref[...] = h


def _post(h, y, g_post, gate, group, n_rows, nxt=None, pack_next=False):
    d = h.shape[1]
    row = pl.BlockSpec((ROW_TILE, d), lambda i: (i, 0))
    vec = pl.BlockSpec((1, d), lambda i: (0, 0))
    per_group = pl.BlockSpec((None, 1, d), lambda i: (group(i), 0, 0))
    in_specs = [row, row, vec, per_group]
    args = [h, y, g_post, gate]
    out_shape = [jax.ShapeDtypeStruct((n_rows, d), F32)]
    out_specs = [row]
    if nxt is not None:
        in_specs += [vec, per_group, per_group]
        args += list(nxt)
        if pack_next:
            out_shape.append(jax.ShapeDtypeStruct((n_rows, d // 2), U32))
            out_specs.append(pl.BlockSpec((ROW_TILE, d // 2), lambda i: (i, 0)))
        else:
            out_shape.append(jax.ShapeDtypeStruct((n_rows, d), BF16))
            out_specs.append(row)
    return pl.pallas_call(
        functools.partial(_post_body, with_next=nxt is not None, pack_next=pack_next),
        out_shape=out_shape,
        grid=(n_rows // ROW_TILE,),
        in_specs=in_specs,
        out_specs=out_specs,
        compiler_params=_params(("parallel",)),
        name="post",
    )(*args)


def _mm_body(*refs, n_a, nk):
    a_refs, b_refs, o_ref = refs[:n_a], refs[n_a:2 * n_a], refs[2 * n_a]
    part = _dot(a_refs[0][...], b_refs[0][...])
    for a_ref, b_ref in zip(a_refs[1:], b_refs[1:]):
        part += _dot(a_ref[...], b_ref[...])
    if nk == 1:
        o_ref[...] = part.astype(o_ref.dtype)
        return
    acc_ref = refs[2 * n_a + 1]
    k = pl.program_id(2)

    @pl.when(k == 0)
    def _():
        acc_ref[...] = part

    @pl.when(k > 0)
    def _():
        acc_ref[...] += part

    @pl.when(k == nk - 1)
    def _():
        o_ref[...] = acc_ref[...].astype(o_ref.dtype)


def _matmul(a_list, b_list, out_dtype, m, tm, tn, tk=None, name="matmul"):
    n = b_list[0].shape[1]
    n_a = len(a_list)
    if tk is None:
        nk = 1
        in_specs = ([pl.BlockSpec((tm, a.shape[1]), lambda i, j: (i, 0)) for a in a_list]
                    + [pl.BlockSpec((b.shape[0], tn), lambda i, j: (0, j)) for b in b_list])
        grid = (m // tm, n // tn)
        out_spec = pl.BlockSpec((tm, tn), lambda i, j: (i, j))
        scratch = []
        sem = ("parallel", "parallel")
    else:
        assert n_a == 1
        kdim = a_list[0].shape[1]
        nk = kdim // tk
        in_specs = [pl.BlockSpec((tm, tk), lambda i, j, k: (i, k)),
                    pl.BlockSpec((tk, tn), lambda i, j, k: (k, j))]
        grid = (m // tm, n // tn, nk)
        out_spec = pl.BlockSpec((tm, tn), lambda i, j, k: (i, j))
        scratch = [pltpu.VMEM((tm, tn), F32)]
        sem = ("parallel", "parallel", "arbitrary")
    return pl.pallas_call(
        functools.partial(_mm_body, n_a=n_a, nk=nk),
        out_shape=jax.ShapeDtypeStruct((m, n), out_dtype),
        grid=grid,
        in_specs=in_specs,
        out_specs=out_spec,
        scratch_shapes=scratch,
        compiler_params=_params(sem),
        name=name,
    )(*a_list, *b_list)


def _glu_body(a_ref, wg_ref, wu_ref, o_ref):
    a = a_ref[...]
    g = _dot(a, wg_ref[...])
    u = _dot(a, wu_ref[...])
    o_ref[...] = (g * _sigmoid(g) * u).astype(o_ref.dtype)


def _glu(a, wg, wu, m, tm, tn):
    n_e, kdim, n = wg.shape
    nj = n // tn
    w_spec = pl.BlockSpec((None, kdim, tn), lambda i, j: (j // nj, 0, j % nj))
    return pl.pallas_call(
        _glu_body,
        out_shape=jax.ShapeDtypeStruct((m, n_e * n), BF16),
        grid=(m // tm, n_e * nj),
        in_specs=[pl.BlockSpec((tm, kdim), lambda i, j: (i, 0)), w_spec, w_spec],
        out_specs=pl.BlockSpec((tm, tn), lambda i, j: (i, j)),
        compiler_params=_params(("parallel", "parallel")),
        name="glu",
    )(a, wg, wu)


def _router_body(u_ref, w_ref, o_ref, sel_ref, *, n_experts):
    lo, hi = _unpack_halves(u_ref[...])
    half = lo.shape[1]
    logits = _dot(lo, w_ref[pl.ds(0, half), :]) + _dot(hi, w_ref[pl.ds(half, half), :])
    lane = lax.broadcasted_iota(jnp.int32, logits.shape, 1)
    l1 = jnp.where(lane < n_experts, logits, NEG)
    m1 = jnp.max(l1, axis=-1, keepdims=True)
    i1 = jnp.min(jnp.where(l1 == m1, lane, LANES), axis=-1, keepdims=True)
    l2 = jnp.where(lane == i1, NEG, l1)
    m2 = jnp.max(l2, axis=-1, keepdims=True)
    i2 = jnp.min(jnp.where(l2 == m2, lane, LANES), axis=-1, keepdims=True)
    e2 = jnp.exp(m2 - m1)
    w1 = 1.0 / (1.0 + e2)
    w2 = e2 / (1.0 + e2)
    is1, is2 = lane == i1, lane == i2
    o_ref[...] = jnp.where(is1, w1, 0.0) + jnp.where(is2, w2, 0.0)
    sel_ref[...] = jnp.where(is1, 1.0, jnp.where(is2, 1.0, 0.0)).astype(sel_ref.dtype)


def _router(u_packed, w_router_padded, m, n_experts):
    half = u_packed.shape[1]
    tm = _tile(m, 1024, ROW_TILE)
    out = pl.BlockSpec((tm, LANES), lambda i: (i, 0))
    return pl.pallas_call(
        functools.partial(_router_body, n_experts=n_experts),
        out_shape=[jax.ShapeDtypeStruct((m, LANES), F32), jax.ShapeDtypeStruct((m, LANES), BF16)],
        grid=(m // tm,),
        in_specs=[pl.BlockSpec((tm, half), lambda i: (i, 0)), pl.BlockSpec((2 * half, LANES), lambda i: (0, 0))],
        out_specs=[out, out],
        compiler_params=_params(("parallel",)),
        name="router",
    )(u_packed, w_router_padded)


def _rank_body(sel_ref, rank_ref, cnt_ref, carry_ref):
    @pl.when(pl.program_id(0) == 0)
    def _():
        carry_ref[...] = jnp.zeros_like(carry_ref)

    sel = sel_ref[...]
    n = sel.shape[0]
    t = lax.broadcasted_iota(jnp.int32, (n, n), 0)
    s = lax.broadcasted_iota(jnp.int32, (n, n), 1)
    before = jnp.where(s < t, 1.0, 0.0).astype(BF16)
    carry = carry_ref[...]
    rank_ref[...] = _dot(before, sel) + carry[0:1, :]
    carry = carry + jnp.sum(sel.astype(F32), axis=0, keepdims=True)
    carry_ref[...] = carry
    cnt_ref[...] = carry


def _rank(sel):
    m = sel.shape[0]
    tile = _tile(m, RANK_TILE, ROW_TILE)
    blk = pl.BlockSpec((tile, LANES), lambda i: (i, 0))
    return pl.pallas_call(
        _rank_body,
        out_shape=[jax.ShapeDtypeStruct((m, LANES), F32), jax.ShapeDtypeStruct((8, LANES), F32)],
        grid=(m // tile,),
        in_specs=[blk],
        out_specs=[blk, pl.BlockSpec((8, LANES), lambda i: (0, 0))],
        scratch_shapes=[pltpu.VMEM((8, LANES), F32)],
        compiler_params=_params(("arbitrary",)),
        name="moe_rank",
    )(sel)


def _slots_body(rank_ref, sel_ref, comb_ref, cnt_ref, pos_ref, wts_ref):
    cnt = cnt_ref[...]
    n_tiles = jnp.floor((cnt + (MOE_TILE - 1)) * (1.0 / MOE_TILE))
    a = lax.broadcasted_iota(jnp.int32, (LANES, LANES), 0)
    b = lax.broadcasted_iota(jnp.int32, (LANES, LANES), 1)
    earlier = jnp.where(a < b, 1.0, 0.0).astype(BF16)
    start = _dot(n_tiles.astype(BF16), earlier)[0:1, :] * MOE_TILE
    sel = sel_ref[...].astype(F32) > 0.5
    lane = lax.broadcasted_iota(jnp.int32, sel.shape, 1)
    lane_a = jnp.min(jnp.where(sel, lane, LANES), axis=-1, keepdims=True)
    lane_b = jnp.max(jnp.where(sel, lane, -1), axis=-1, keepdims=True)
    row = start + rank_ref[...]
    comb = comb_ref[...]

    def pick(x, which):
        return jnp.sum(jnp.where(lane == which, x, 0.0), axis=-1, keepdims=True)

    pos_ref[...] = jnp.where(lane == 0, pick(row, lane_a), jnp.where(lane == 1, pick(row, lane_b), 0.0)).astype(jnp.int32)
    wts_ref[...] = jnp.where(lane == 0, pick(comb, lane_a), jnp.where(lane == 1, pick(comb, lane_b), 0.0))


def _slots(rank, sel, comb, counts):
    m = rank.shape[0]
    tm = _tile(m, 1024, ROW_TILE)
    blk = pl.BlockSpec((tm, LANES), lambda i: (i, 0))
    return pl.pallas_call(
        _slots_body,
        out_shape=[jax.ShapeDtypeStruct((m, LANES), jnp.int32), jax.ShapeDtypeStruct((m, LANES), F32)],
        grid=(m // tm,),
        in_specs=[blk, blk, blk, pl.BlockSpec((8, LANES), lambda i: (0, 0))],
        out_specs=[blk, blk],
        compiler_params=_params(("parallel",)),
        name="moe_slots",
    )(rank, sel, comb, counts)


def _row_copy(src_ref, src_row, dst_ref, dst_row, sem):
    return pltpu.make_async_copy(src_ref.at[pl.ds(src_row, 1)], dst_ref.at[pl.ds(dst_row, 1)], sem)


def _invert_body(pa_ref, pb_ref, src_ref, *, n_tokens):
    def clear(r, carry):
        src_ref[r] = 0
        return carry

    lax.fori_loop(0, src_ref.shape[0], clear, 0, unroll=SCALAR_UNROLL)

    def scatter(t, carry):
        src_ref[pa_ref[t]] = t
        src_ref[pb_ref[t]] = t
        return carry

    lax.fori_loop(0, n_tokens, scatter, 0, unroll=SCALAR_UNROLL)


def _invert(pos_a, pos_b, n_sorted):
    grid_spec = pltpu.PrefetchScalarGridSpec(
        num_scalar_prefetch=2, grid=(1,), in_specs=[],
        out_specs=pl.BlockSpec(memory_space=pltpu.SMEM))
    return pl.pallas_call(
        functools.partial(_invert_body, n_tokens=pos_a.shape[0]),
        out_shape=jax.ShapeDtypeStruct((n_sorted,), jnp.int32),
        grid_spec=grid_spec,
        compiler_params=_params(("arbitrary",)),
        name="moe_invert",
    )(pos_a, pos_b)


def _dispatch_body(src_ref, u_ref, xs_ref, sem):
    i = pl.program_id(0)
    tile = xs_ref.shape[0]

    def start(r, carry):
        _row_copy(u_ref, src_ref[i * tile + r], xs_ref, r, sem).start()
        return carry

    lax.fori_loop(0, tile, start, 0, unroll=SCALAR_UNROLL)

    def wait(r, carry):
        _row_copy(u_ref, 0, xs_ref, 0, sem).wait()
        return carry

    lax.fori_loop(0, tile, wait, 0, unroll=SCALAR_UNROLL)


def _dispatch(src, u_packed):
    n_sorted = src.shape[0]
    half = u_packed.shape[1]
    grid_spec = pltpu.PrefetchScalarGridSpec(
        num_scalar_prefetch=1, grid=(n_sorted // MOE_TILE,),
        in_specs=[pl.BlockSpec(memory_space=pl.ANY)],
        out_specs=pl.BlockSpec((MOE_TILE, half), lambda i, src: (i, 0)),
        scratch_shapes=[pltpu.SemaphoreType.DMA(())])
    return pl.pallas_call(
        _dispatch_body,
        out_shape=jax.ShapeDtypeStruct((n_sorted, half), U32),
        grid_spec=grid_spec,
        compiler_params=_params(("arbitrary",)),
        name="moe_dispatch",
    )(src, u_packed)


def _new_expert(te_ref):
    i = pl.program_id(1)
    return jnp.logical_or(i == 0, te_ref[i] != te_ref[jnp.maximum(i - 1, 0)])


def _expert_glu_body(te_ref, x_ref, wg_ref, wu_ref, o_ref, wg_s, wu_s):
    @pl.when(_new_expert(te_ref))
    def _():
        wg_s[...] = wg_ref[...].astype(BF16)
        wu_s[...] = wu_ref[...].astype(BF16)

    lo, hi = _unpack_halves(x_ref[...])
    half = lo.shape[1]
    g = _dot(lo, wg_s[pl.ds(0, half), :]) + _dot(hi, wg_s[pl.ds(half, half), :])
    u = _dot(lo, wu_s[pl.ds(0, half), :]) + _dot(hi, wu_s[pl.ds(half, half), :])
    o_ref[...] = (g * _sigmoid(g) * u).astype(o_ref.dtype)


def _expert_glu(tile_expert, xs, wg, wu, layer, tn):
    n_sorted, half = xs.shape
    _, _, kdim, n = wg.shape
    w_spec = pl.BlockSpec((None, None, kdim, tn), lambda j, i, te: (layer, te[i], 0, j))
    grid_spec = pltpu.PrefetchScalarGridSpec(
        num_scalar_prefetch=1, grid=(n // tn, n_sorted // MOE_TILE),
        in_specs=[pl.BlockSpec((MOE_TILE, half), lambda j, i, te: (i, 0)), w_spec, w_spec],
        out_specs=pl.BlockSpec((MOE_TILE, tn), lambda j, i, te: (i, j)),
        scratch_shapes=[pltpu.VMEM((kdim, tn), BF16), pltpu.VMEM((kdim, tn), BF16)])
    return pl.pallas_call(
        _expert_glu_body,
        out_shape=jax.ShapeDtypeStruct((n_sorted, n), BF16),
        grid_spec=grid_spec,
        compiler_params=_params(("parallel", "arbitrary")),
        name="moe_glu",
    )(tile_expert, xs, wg, wu)


def _expert_down_body(te_ref, h_ref, w_ref, o_ref, w_s):
    @pl.when(_new_expert(te_ref))
    def _():
        w_s[...] = w_ref[...].astype(BF16)

    o_ref[...] = _dot(h_ref[...], w_s[...])


def _expert_down(tile_expert, hs, wd, layer, tn):
    n_sorted, kdim = hs.shape
    n = wd.shape[3]
    grid_spec = pltpu.PrefetchScalarGridSpec(
        num_scalar_prefetch=1, grid=(n // tn, n_sorted // MOE_TILE),
        in_specs=[pl.BlockSpec((MOE_TILE, kdim), lambda j, i, te: (i, 0)),
                  pl.BlockSpec((None, None, kdim, tn), lambda j, i, te: (layer, te[i], 0, j))],
        out_specs=pl.BlockSpec((MOE_TILE, tn), lambda j, i, te: (i, j)),
        scratch_shapes=[pltpu.VMEM((kdim, tn), BF16)])
    return pl.pallas_call(
        _expert_down_body,
        out_shape=jax.ShapeDtypeStruct((n_sorted, n), F32),
        grid_spec=grid_spec,
        compiler_params=_params(("parallel", "arbitrary")),
        name="moe_down",
    )(tile_expert, hs, wd)


def _combine_post_body(pa_ref, pb_ref, os_ref, h_ref, wts_ref, gp_ref, gate_ref, ho_ref, buf, sems):
    i = pl.program_id(0)
    n = pl.num_programs(0)
    tile = h_ref.shape[0]
    slot = i % 2

    def issue(tile_idx, dst_slot):
        def one(t, carry):
            tok = tile_idx * tile + t
            _row_copy(os_ref, pa_ref[tok], buf.at[dst_slot, 0], t, sems.at[dst_slot]).start()
            _row_copy(os_ref, pb_ref[tok], buf.at[dst_slot, 1], t, sems.at[dst_slot]).start()
            return carry
        lax.fori_loop(0, tile, one, 0, unroll=SCALAR_UNROLL)

    @pl.when(i == 0)
    def _():
        issue(0, 0)

    @pl.when(i + 1 < n)
    def _():
        issue(i + 1, 1 - slot)

    def wait_one(t, carry):
        _row_copy(os_ref, 0, buf.at[slot, 0], 0, sems.at[slot]).wait()
        return carry

    lax.fori_loop(0, 2 * tile, wait_one, 0, unroll=SCALAR_UNROLL)
    wts = wts_ref[...]
    y = wts[:, 0:1] * buf[slot, 0] + wts[:, 1:2] * buf[slot, 1]
    ho_ref[...] = h_ref[...] + gate_ref[...] * _rms(y, gp_ref[...])


def _combine_post(pos_a, pos_b, os, h, wts, g_post, gate, group, n_rows):
    d = h.shape[1]
    row = lambda i, pa, pb: (i, 0)
    grid_spec = pltpu.PrefetchScalarGridSpec(
        num_scalar_prefetch=2, grid=(n_rows // ROW_TILE,),
        in_specs=[pl.BlockSpec(memory_space=pl.ANY),
                  pl.BlockSpec((ROW_TILE, d), row),
                  pl.BlockSpec((ROW_TILE, LANES), row),
                  pl.BlockSpec((1, d), lambda i, pa, pb: (0, 0)),
                  pl.BlockSpec((None, 1, d), lambda i, pa, pb: (group(i), 0, 0))],
        out_specs=pl.BlockSpec((ROW_TILE, d), row),
        scratch_shapes=[pltpu.VMEM((2, 2, ROW_TILE, d), F32), pltpu.SemaphoreType.DMA((2,))])
    return pl.pallas_call(
        _combine_post_body,
        out_shape=jax.ShapeDtypeStruct((n_rows, d), F32),
        grid_spec=grid_spec,
        compiler_params=_params(("arbitrary",)),
        name="moe_combine_post",
    )(pos_a, pos_b, os, h, wts, g_post, gate)


def _split3(x):
    hi = x.astype(BF16)
    r1 = x - hi.astype(F32)
    mid = r1.astype(BF16)
    lo = (r1 - mid.astype(F32)).astype(BF16)
    return hi, mid, lo


def _gates_body(g_ref, b_ref, o_ref, *, n_heads):
    pre = g_ref[...] + b_ref[...]
    log_f = jnp.minimum(pre, 0.0) - jnp.log(1.0 + jnp.exp(-jnp.abs(pre)))
    lc = pre.shape[0]
    t = lax.broadcasted_iota(jnp.int32, (lc, lc), 0)
    s = lax.broadcasted_iota(jnp.int32, (lc, lc), 1)
    tri_f = jnp.where(s <= t, 1.0, 0.0).astype(BF16)
    tri_b = jnp.where(s >= t, 1.0, 0.0).astype(BF16)
    hi, mid, lo = _split3(log_f)
    cs_f = _dot(tri_f, hi) + _dot(tri_f, mid) + _dot(tri_f, lo)
    cs_b = _dot(tri_b, hi) + _dot(tri_b, mid) + _dot(tri_b, lo)
    lane = lax.broadcasted_iota(jnp.int32, pre.shape, 1)
    kind = lane // n_heads
    a_f = pre - pltpu.roll(cs_f, LANES - n_heads, 1)
    a_b = pre - pltpu.roll(cs_b, LANES - n_heads, 1)
    o_ref[...] = jnp.where(kind == 0, a_f, jnp.where(kind == 1, cs_f, jnp.where(kind == 2, a_b, cs_b)))


def _gates(g_raw, bias_row, n_heads):
    t_all = g_raw.shape[0]
    blk = pl.BlockSpec((MLSTM_CHUNK, LANES), lambda i: (i, 0))
    return pl.pallas_call(
        functools.partial(_gates_body, n_heads=n_heads),
        out_shape=jax.ShapeDtypeStruct((t_all, LANES), F32),
        grid=(t_all // MLSTM_CHUNK,),
        in_specs=[blk, pl.BlockSpec((1, LANES), lambda i: (0, 0))],
        out_specs=blk,
        compiler_params=_params(("parallel",)),
        name="mlstm_gates",
    )(g_raw, bias_row)


STATE_ROWS = HEAD_DIM + 16


def _mlstm_chunk(q, k, vte, a_col, a_row, b_row, state, rev):
    ct, m_st = state
    lc = q.shape[0]
    b_last = b_row[:, 0:1] if rev else b_row[:, lc - 1:lc]
    m_new = jnp.maximum(b_last + m_st, b_last + jnp.max(a_row, axis=-1, keepdims=True))
    decay = jnp.exp(b_last + m_st - m_new)
    w_row = jnp.exp(b_last + a_row - m_new)
    lhs = (vte.astype(F32) * w_row).astype(BF16)
    ct_new = decay * ct + _dot(lhs, k)

    j = lax.broadcasted_iota(jnp.int32, (lc, lc), 0)
    t = lax.broadcasted_iota(jnp.int32, (lc, lc), 1)
    seen = (j >= t) if rev else (j <= t)
    dmat = jnp.where(seen, a_col + b_row, NEG)
    m_inter = b_row + m_st
    m_t = jnp.maximum(m_inter, jnp.max(dmat, axis=0, keepdims=True))
    s = _dot_nt(k, q) * jnp.exp(dmat - m_t)
    inter = jnp.exp(m_inter - m_t)
    from_state = _dot_nt(ct.astype(BF16), q)
    num = _dot(vte, s.astype(BF16))[:HEAD_DIM] + inter * from_state[:HEAD_DIM]
    den = jnp.sum(s, axis=0, keepdims=True) + inter * from_state[HEAD_DIM:HEAD_DIM + 1]
    h = num / jnp.maximum(jnp.abs(den), jnp.exp(-m_t))
    return h, (ct_new, m_new)


def _mlstm_body(qc_ref, kc_ref, vtc_ref, oc_ref, ql_ref, kl_ref, vtl_ref, ol_ref,
                acf_ref, acb_ref, grc_ref, alf_ref, alb_ref, grl_ref, cos_ref, sin_ref, ng_ref, *rest, with_ctx_out):
    if with_ctx_out:
        yl_ref, yc_ref, qs_ref, ks_ref, vte_ref, hf_ref, hb_ref = rest
    else:
        yl_ref, qs_ref, ks_ref, vte_ref, hf_ref, hb_ref = rest
    lc = MLSTM_CHUNK
    n_lat = ql_ref.shape[0] // lc
    n_ctx = qc_ref.shape[0] // lc
    scale = HEAD_DIM ** -0.5
    pad_row = lax.broadcasted_iota(jnp.int32, (STATE_ROWS - HEAD_DIM, lc), 0)
    ones_pad = jnp.where(pad_row == 0, 1.0, 0.0).astype(BF16)
    quarter = HEAD_DIM // 4
    src = lax.broadcasted_iota(jnp.int32, (HEAD_DIM, HEAD_DIM), 0)
    dst = lax.broadcasted_iota(jnp.int32, (HEAD_DIM, HEAD_DIM), 1)
    rot = jnp.where((dst % (2 * quarter)) < quarter,
                    jnp.where(src == dst + quarter, -1.0, 0.0),
                    jnp.where(src == dst - quarter, 1.0, 0.0)).astype(BF16)

    def rope(x, rows):
        return x.astype(F32) * cos_ref[rows, :] + _dot(x, rot) * sin_ref[rows, :]

    def prep_chunk(i, carry):
        rows = pl.ds(pl.multiple_of(i * lc, lc), lc)
        qs_ref[rows, :] = (rope(ql_ref[rows, :], rows) * scale).astype(BF16)
        ks_ref[rows, :] = rope(kl_ref[rows, :], rows).astype(BF16)
        vte_ref[pl.ds(0, HEAD_DIM), rows] = vtl_ref[:, rows]
        vte_ref[pl.ds(HEAD_DIM, STATE_ROWS - HEAD_DIM), rows] = ones_pad
        return carry

    lax.fori_loop(0, n_lat, prep_chunk, 0)

    ng = ng_ref[...]

    def finish(h_t, o_gate):
        y_t = h_t * lax.rsqrt(jnp.mean(h_t * h_t, axis=0, keepdims=True) + EPS)
        return (_sigmoid(o_gate.astype(F32)) * (y_t.T * ng)).astype(BF16)

    zero = (jnp.zeros((STATE_ROWS, HEAD_DIM), F32), jnp.zeros((1, 1), F32))
    a_cols_c, a_cols_l = (acf_ref, acb_ref), (alf_ref, alb_ref)

    st = [zero, zero]
    for i in range(n_ctx):
        for d in range(2):
            ci = i if d == 0 else n_ctx - 1 - i
            rows = pl.ds(ci * lc, lc)
            q = (qc_ref[rows, :].astype(F32) * scale).astype(BF16)
            vte = jnp.concatenate([vtc_ref[:, rows], ones_pad], axis=0)
            h, st[d] = _mlstm_chunk(q, kc_ref[rows, :], vte, a_cols_c[d][rows, :], grc_ref[pl.ds(2 * d, 1), rows],
                                    grc_ref[pl.ds(2 * d + 1, 1), rows], st[d], d == 1)
            if with_ctx_out:
                (hf_ref if d == 0 else hb_ref)[:, rows] = h
    if with_ctx_out:
        for i in range(n_ctx):
            rows = pl.ds(i * lc, lc)
            yc_ref[rows, :] = finish(hf_ref[:, rows] + hb_ref[:, rows], oc_ref[rows, :])

    def lat_step(i, carry):
        out = []
        for d, st_d in enumerate(carry):
            ci = i if d == 0 else n_lat - 1 - i
            rows = pl.ds(pl.multiple_of(ci * lc, lc), lc)
            h, st_d = _mlstm_chunk(qs_ref[rows, :], ks_ref[rows, :], vte_ref[:, rows], a_cols_l[d][rows, :],
                                   grl_ref[pl.ds(2 * d, 1), rows], grl_ref[pl.ds(2 * d + 1, 1), rows], st_d, d == 1)
            (hf_ref if d == 0 else hb_ref)[:, rows] = h
            out.append(st_d)
        return tuple(out)

    lax.fori_loop(0, n_lat, lat_step, (st[0], st[1]))

    def finish_chunk(i, carry):
        rows = pl.ds(pl.multiple_of(i * lc, lc), lc)
        yl_ref[rows, :] = finish(hf_ref[:, rows] + hb_ref[:, rows], ol_ref[rows, :])
        return carry

    lax.fori_loop(0, n_lat, finish_chunk, 0)


def _mlstm(p, v_t, a_col, g_row, cos, sin, norm_g, n_batch, seq, ctx_len, n_heads, with_ctx_out):
    ctx_blk0 = n_batch * seq // ctx_len

    def col(kind, lat):
        if lat:
            return pl.BlockSpec((seq, HEAD_DIM), lambda b, h: (b, kind * n_heads + h))
        return pl.BlockSpec((ctx_len, HEAD_DIM), lambda b, h: (ctx_blk0 + b, kind * n_heads + h))

    def a_spec(d, lat):
        if lat:
            return pl.BlockSpec((None, None, seq, 1), lambda b, h: (h, d, b, 0))
        return pl.BlockSpec((None, None, ctx_len, 1), lambda b, h: (h, d, ctx_blk0 + b, 0))

    in_specs = [
        col(0, False), col(1, False), pl.BlockSpec((HEAD_DIM, ctx_len), lambda b, h: (h, ctx_blk0 + b)), col(3, False),
        col(0, True), col(1, True), pl.BlockSpec((HEAD_DIM, seq), lambda b, h: (h, b)), col(3, True),
        a_spec(0, False), a_spec(1, False), pl.BlockSpec((None, 4, ctx_len), lambda b, h: (h, 0, ctx_blk0 + b)),
        a_spec(0, True), a_spec(1, True), pl.BlockSpec((None, 4, seq), lambda b, h: (h, 0, b)),
        pl.BlockSpec((seq, HEAD_DIM), lambda b, h: (0, 0)),
        pl.BlockSpec((seq, HEAD_DIM), lambda b, h: (0, 0)),
        pl.BlockSpec((None, 1, HEAD_DIM), lambda b, h: (h, 0, 0)),
    ]
    out_shape = [jax.ShapeDtypeStruct((n_batch * seq, n_heads * HEAD_DIM), BF16)]
    out_specs = [pl.BlockSpec((seq, HEAD_DIM), lambda b, h: (b, h))]
    if with_ctx_out:
        out_shape.append(jax.ShapeDtypeStruct((n_batch * ctx_len, n_heads * HEAD_DIM), BF16))
        out_specs.append(pl.BlockSpec((ctx_len, HEAD_DIM), lambda b, h: (b, h)))
    return pl.pallas_call(
        functools.partial(_mlstm_body, with_ctx_out=with_ctx_out),
        out_shape=out_shape,
        grid=(n_batch, n_heads),
        in_specs=in_specs,
        out_specs=out_specs,
        scratch_shapes=[pltpu.VMEM((seq, HEAD_DIM), BF16), pltpu.VMEM((seq, HEAD_DIM), BF16),
                        pltpu.VMEM((STATE_ROWS, seq), BF16),
                        pltpu.VMEM((HEAD_DIM, seq), F32), pltpu.VMEM((HEAD_DIM, seq), F32)],
        compiler_params=_params(("parallel", "parallel")),
        name="mlstm",
    )(p, p, v_t, p, p, p, v_t, p, a_col, a_col, g_row, a_col, a_col, g_row, cos, sin, norm_g)


def _conv_body(ap_ref, gp_ref, a_ref, g_ref, an_ref, gn_ref, w_ref, b_ref, ng_ref, o_ref, u_ref, c_ref, s_ref,
               *, n_lat_tiles, lat_tiles_per_seq, width):
    i = pl.program_id(0)
    is_lat = i < n_lat_tiles
    pos = i % lat_tiles_per_seq
    first = jnp.logical_or(jnp.logical_not(is_lat), pos == 0)
    last = jnp.logical_or(jnp.logical_not(is_lat), pos == lat_tiles_per_seq - 1)

    def glu(x_ref, y_ref):
        return x_ref[...].astype(F32) * _sigmoid(y_ref[...].astype(F32))

    tile = a_ref.shape[0]
    u_ref[pl.ds(0, HALO), :] = jnp.where(first, 0.0, 1.0) * glu(ap_ref, gp_ref)
    u_ref[pl.ds(HALO, tile), :] = glu(a_ref, g_ref)
    u_ref[pl.ds(HALO + tile, HALO), :] = jnp.where(last, 0.0, 1.0) * glu(an_ref, gn_ref)

    strip, cblk = 32, 256
    n_ch = a_ref.shape[1]
    cblk = min(cblk, n_ch)
    base = HALO - width // 2
    span = s_ref.shape[1]
    for phase in range(1, SUBLANES):
        s_ref[phase] = u_ref[pl.ds(phase, span), :]

    for r0 in range(0, tile, strip):
        for cb in range(n_ch // cblk):
            cols = pl.ds(cb * cblk, cblk)
            acc = jnp.zeros((strip, cblk), F32)
            for k in range(width):
                phase, row = (base + k) % SUBLANES, r0 + (base + k) // SUBLANES * SUBLANES
                tap = u_ref[pl.ds(row, strip), cols] if phase == 0 else s_ref[phase, pl.ds(row, strip), cols]
                acc += w_ref[pl.ds(k, 1), cols] * tap
            c_ref[pl.ds(r0, strip), cols] = acc
    y = _rms(c_ref[...] + b_ref[...], ng_ref[...])
    o_ref[...] = (y * _sigmoid(y)).astype(o_ref.dtype)


def _conv(p, conv_w, conv_b, conv_norm, a_col_blk, n_rows, n_lat_tiles, lat_tiles_per_seq):
    width, n_ch = conv_w.shape
    assert width // 2 <= HALO
    per = ROW_TILE // HALO
    last_halo = p.shape[0] // HALO - 1

    def cur(c):
        return pl.BlockSpec((ROW_TILE, n_ch), lambda i: (i, c))

    def prev(c):
        return pl.BlockSpec((HALO, n_ch), lambda i: (jnp.maximum(i * per - 1, 0), c))

    def nxt(c):
        return pl.BlockSpec((HALO, n_ch), lambda i: (jnp.minimum((i + 1) * per, last_halo), c))

    vec = pl.BlockSpec((1, n_ch), lambda i: (0, 0))
    a, g = a_col_blk, a_col_blk + 1
    return pl.pallas_call(
        functools.partial(_conv_body, n_lat_tiles=n_lat_tiles, lat_tiles_per_seq=lat_tiles_per_seq, width=width),
        out_shape=jax.ShapeDtypeStruct((n_rows, n_ch), BF16),
        grid=(n_rows // ROW_TILE,),
        in_specs=[prev(a), prev(g), cur(a), cur(g), nxt(a), nxt(g),
                  pl.BlockSpec((width, n_ch), lambda i: (0, 0)), vec, vec],
        out_specs=pl.BlockSpec((ROW_TILE, n_ch), lambda i: (i, 0)),
        scratch_shapes=[pltpu.VMEM((ROW_TILE + 2 * HALO, n_ch), F32), pltpu.VMEM((ROW_TILE, n_ch), F32),
                        pltpu.VMEM((SUBLANES, ROW_TILE + 2 * HALO - SUBLANES, n_ch), F32)],
        compiler_params=_params(("parallel",)),
        name="conformer_conv",
    )(p, p, p, p, p, p, conv_w, conv_b, conv_norm)


def _na_plan(rows):
    assert rows % NA_GROUP == 0 and rows >= NA_UNION
    plans = []
    for g in range(rows // NA_GROUP):
        union_start = min(max(g * NA_GROUP - NA_ROWS // 2, 0), rows - NA_UNION)
        plan = []
        for r in range(g * NA_GROUP, (g + 1) * NA_GROUP):
            row_start = min(max(r - NA_ROWS // 2, 0), rows - NA_ROWS)
            offset = row_start - union_start
            first_tile = row_start - r + NA_ROWS - 1
            assert 0 <= offset <= NA_UNION - NA_ROWS and first_tile - offset + NA_PAD_TILES >= 0
            plan.append((first_tile - offset + NA_PAD_TILES, offset))
        plans.append(tuple(plan))
    assert all(p == plans[1] for p in plans[1:-1])
    return plans[0], plans[1], plans[-1]


def _na_body(*refs, rows, with_ctx_out):
    if with_ctx_out:
        q_ref, k_ref, v_ref, kc_ref, vc_ref, bias_ref, qc_ref, o_ref, oc_ref = refs
    else:
        q_ref, k_ref, v_ref, kc_ref, vc_ref, bias_ref, o_ref = refs
    scale = HEAD_DIM ** -0.5
    kc = kc_ref[...]
    vc = vc_ref[...]
    n_groups = rows // NA_GROUP

    def row_group(g, carry):
        rows_g = pl.ds(pl.multiple_of(g * (NA_GROUP * GRID_W), NA_GROUP * GRID_W), NA_GROUP * GRID_W)
        union_start = jnp.clip(g * NA_GROUP - NA_ROWS // 2, 0, rows - NA_UNION)
        win = pl.ds(pl.multiple_of(union_start * GRID_W, GRID_W), NA_UNION * GRID_W)
        variant = jnp.where(g == 0, 0, jnp.where(g == n_groups - 1, 2, 1))
        q_g = q_ref[rows_g, :]
        s_win = _dot_nt(q_g, k_ref[win, :]) * scale + bias_ref[variant]
        s_ctx = _dot_nt(q_g, kc) * scale
        m = jnp.maximum(jnp.max(s_win, axis=-1, keepdims=True), jnp.max(s_ctx, axis=-1, keepdims=True))
        p_win = jnp.exp(s_win - m)
        p_ctx = jnp.exp(s_ctx - m)
        denom = jnp.sum(p_win, axis=-1, keepdims=True) + jnp.sum(p_ctx, axis=-1, keepdims=True)
        out = (_dot(p_win.astype(BF16), v_ref[win, :]) + _dot(p_ctx.astype(BF16), vc)) / denom
        o_ref[rows_g, :] = out.astype(o_ref.dtype)
        return carry

    lax.fori_loop(0, n_groups, row_group, 0, unroll=2)

    if with_ctx_out:
        s = _dot_nt(qc_ref[...], kc) * scale
        p = jnp.exp(s - jnp.max(s, axis=-1, keepdims=True))
        oc_ref[...] = (_dot(p.astype(BF16), vc) / jnp.sum(p, axis=-1, keepdims=True)).astype(oc_ref.dtype)


def _na(p, bias, col0, n_batch, seq, ctx_len, n_heads, with_ctx_out):
    ctx_blk0 = n_batch * seq // ctx_len
    rows = seq // GRID_W

    def lat(kind):
        return pl.BlockSpec((seq, HEAD_DIM), lambda b, h: (b, col0 + kind * n_heads + h))

    def ctx(kind):
        return pl.BlockSpec((ctx_len, HEAD_DIM), lambda b, h: (ctx_blk0 + b, col0 + kind * n_heads + h))

    in_specs = [lat(0), lat(1), lat(2), ctx(1), ctx(2),
                pl.BlockSpec((None,) + bias.shape[1:], lambda b, h: (h, 0, 0, 0))]
    args = [p, p, p, p, p, bias]
    out_shape = [jax.ShapeDtypeStruct((n_batch * seq, n_heads * HEAD_DIM), BF16)]
    out_specs = [pl.BlockSpec((seq, HEAD_DIM), lambda b, h: (b, h))]
    if with_ctx_out:
        in_specs.append(ctx(0))
        args.append(p)
        out_shape.append(jax.ShapeDtypeStruct((n_batch * ctx_len, n_heads * HEAD_DIM), BF16))
        out_specs.append(pl.BlockSpec((ctx_len, HEAD_DIM), lambda b, h: (b, h)))
    return pl.pallas_call(
        functools.partial(_na_body, rows=rows, with_ctx_out=with_ctx_out),
        out_shape=out_shape,
        grid=(n_batch, n_heads),
        in_specs=in_specs,
        out_specs=out_specs,
        compiler_params=_params(("parallel", "parallel")),
        name="neighbourhood_attention",
    )(*args)


def _na_bias_body(r_ref, o_ref, *, plans):
    n_taps = 2 * NA_COLS - 1
    width = r_ref.shape[1]
    q = lax.broadcasted_iota(jnp.int32, (GRID_W, width), 0)
    k = lax.broadcasted_iota(jnp.int32, (GRID_W, width), 1) % GRID_W
    tap = k - q + (NA_COLS - 1)
    strip = jnp.zeros((GRID_W, width), F32)
    for j in range(n_taps):
        strip = jnp.where(tap == j, r_ref[pl.ds(j, 1), :], strip)
    col_start = jnp.clip(q - NA_COLS // 2, 0, GRID_W - NA_COLS)
    strip = jnp.where(k >= col_start, jnp.where(k < col_start + NA_COLS, strip, NEG), NEG)
    win = NA_UNION * GRID_W
    key_row = lax.broadcasted_iota(jnp.int32, (GRID_W, win), 1) // GRID_W
    for v, plan in enumerate(plans):
        for i, (shift, offset) in enumerate(plan):
            piece = strip[:, shift * GRID_W:shift * GRID_W + win]
            piece = jnp.where(key_row >= offset, jnp.where(key_row < offset + NA_ROWS, piece, NEG), NEG)
            o_ref[v, pl.ds(i * GRID_W, GRID_W), :] = piece


def _na_bias_table(rpb, rows):
    n_heads, n_dr, n_taps = rpb.shape
    assert n_dr == 2 * NA_ROWS - 1 and n_taps == 2 * NA_COLS - 1
    plans = _na_plan(rows)
    n_tiles = max(shift for plan in plans for shift, _ in plan) + NA_UNION
    width = -(-(n_tiles * GRID_W) // LANES) * LANES
    spread = jnp.repeat(rpb.transpose(0, 2, 1), GRID_W, axis=2)
    lead = NA_PAD_TILES * GRID_W
    spread = jnp.pad(spread, ((0, 0), (0, 32 - n_taps), (lead, width - lead - n_dr * GRID_W)))
    out_blk = (len(plans), NA_GROUP * GRID_W, NA_UNION * GRID_W)
    return pl.pallas_call(
        functools.partial(_na_bias_body, plans=plans),
        out_shape=jax.ShapeDtypeStruct((n_heads,) + out_blk, F32),
        grid=(n_heads,),
        in_specs=[pl.BlockSpec((None, 32, width), lambda h: (h, 0, 0))],
        out_specs=pl.BlockSpec((None,) + out_blk, lambda h: (h, 0, 0, 0)),
        compiler_params=_params(("parallel",)),
        name="na_bias",
    )(spread)


def _rope_tables(length):
    t = jnp.arange(length)
    n_freq = HEAD_DIM // 4
    freqs = ROPE_BASE ** (-jnp.arange(n_freq, dtype=F32) / n_freq)
    ang_r = (t // GRID_W).astype(F32)[:, None] * freqs
    ang_c = (t % GRID_W).astype(F32)[:, None] * freqs
    ang = jnp.concatenate([ang_r, ang_r, ang_c, ang_c], axis=-1)
    return jnp.cos(ang), jnp.sin(ang)


def kernel(x, c, ctx, c_ctx, w_mod, b_mod, g_norm, w_in, w_out, mlstm_gate_bias, mlstm_norm, conv_w, conv_b,
           conv_norm, na_rpb, ffn_w_gate, ffn_w_up, ffn_w_down, router_w, moe_w_gate, moe_w_up, moe_w_down):
    n_batch, seq, d = x.shape
    ctx_len = ctx.shape[1]
    depth = w_mod.shape[0]
    m_width = mlstm_norm.shape[1]
    c_width = conv_w.shape[2]
    n_width = d - m_width - c_width
    mh, nh = m_width // HEAD_DIM, n_width // HEAD_DIM
    n_experts = router_w.shape[2]
    t_lat, t_ctx = n_batch * seq, n_batch * ctx_len
    t_all = t_lat + t_ctx
    assert seq % MLSTM_CHUNK == 0 and ctx_len % MLSTM_CHUNK == 0 and ctx_len == ROW_TILE
    assert t_lat % ctx_len == 0 and (4 * m_width) % c_width == 0 and 4 * mh <= LANES
    n_lat_tiles, lat_tiles_per_seq = t_lat // ROW_TILE, seq // ROW_TILE
    group = _group_map(n_lat_tiles, lat_tiles_per_seq, n_batch)

    h = jnp.concatenate([x.reshape(t_lat, d), ctx.reshape(t_ctx, d)], axis=0)
    cond = jnp.concatenate([c, jnp.broadcast_to(c_ctx[None], (8 - n_batch, d))], axis=0)
    mod = _modulation(cond, w_mod, b_mod)
    grp_rows = jnp.concatenate([jnp.arange(n_batch), jnp.full((n_batch,), n_batch)])
    cos, sin = _rope_tables(seq)
    tm_all, tm_lat = _tile(t_all, 1024, ROW_TILE), _tile(t_lat, 1024, ROW_TILE)

    def mods(layer):
        m6 = mod[layer][grp_rows].reshape(2 * n_batch, N_MOD, 1, d)
        return [m6[:, k] for k in range(N_MOD)]

    gv = lambda layer, k: g_norm[layer, k].reshape(1, d)

    sh1, sc1, _, _, _, _ = mods(0)
    u = _normmod(h, gv(0, 0), sh1, sc1, group, t_all)

    for layer in range(depth):
        last = layer == depth - 1
        with_ctx_out = not last
        sh1, sc1, gt1, sh2, sc2, gt2 = mods(layer)
        rows_out = t_lat if last else t_all
        tm_out = tm_lat if last else tm_all

        w = w_in[layer]
        cut_g, cut_c = 4 * m_width, 4 * m_width + 4 * mh
        w_main = jnp.concatenate([w[:, :cut_g], w[:, cut_c:]], axis=1).astype(BF16)
        w_gate = jnp.pad(w[:, cut_g:cut_c], ((0, 0), (0, LANES - 4 * mh))).astype(BF16)
        n_main = w_main.shape[1]
        p = _matmul([u], [w_main], BF16, t_all, tm_all, _tile(n_main, 1280, 2 * LANES), name="w_in")
        g_raw = _matmul([u], [w_gate], F32, t_all, tm_all, LANES, name="w_in_gates")

        bias_row = jnp.pad(mlstm_gate_bias[layer].reshape(1, 4 * mh), ((0, 0), (0, LANES - 4 * mh)))
        g_tab = _gates(g_raw, bias_row, mh)[:, :4 * mh].reshape(t_all, 4, mh)
        g_row = g_tab.transpose(2, 1, 0)
        a_col = g_row[:, 0::2, :, None]
        v_t = p[:, 2 * m_width:3 * m_width].T
        ym = _mlstm(p, v_t, a_col, g_row, cos, sin, mlstm_norm[layer].reshape(mh, 1, HEAD_DIM),
                    n_batch, seq, ctx_len, mh, with_ctx_out)
        yc = _conv(p, conv_w[layer], conv_b[layer].reshape(1, c_width), conv_norm[layer].reshape(1, c_width),
                   cut_g // c_width, rows_out, n_lat_tiles, lat_tiles_per_seq)
        bias = _na_bias_table(na_rpb[layer], seq // GRID_W)
        yn = _na(p, bias, (cut_g + 2 * c_width) // HEAD_DIM, n_batch, seq, ctx_len, nh, with_ctx_out)
        if with_ctx_out:
            ym, yn = jnp.concatenate(ym, axis=0), jnp.concatenate(yn, axis=0)
        else:
            ym, yn = ym[0], yn[0]

        wo = w_out[layer]
        wo_parts = [wo[:m_width].astype(BF16), wo[m_width:m_width + c_width].astype(BF16),
                    wo[m_width + c_width:].astype(BF16)]
        mix = _matmul([ym, yc, yn], wo_parts, F32, rows_out, tm_out, _tile(d, 1024, 2 * LANES), name="w_out")
        moe = layer % 2 == 1
        h, u2 = _post(h, mix, gv(layer, 1), gt1, group, rows_out, nxt=(gv(layer, 2), sh2, sc2), pack_next=moe)

        j = layer // 2
        if not moe:
            d_ff = ffn_w_gate.shape[2]
            hid = _glu(u2, ffn_w_gate[j][None].astype(BF16), ffn_w_up[j][None].astype(BF16),
                       rows_out, tm_out, _tile(d_ff, 512, 2 * LANES))
            f = _matmul([hid], [ffn_w_down[j].astype(BF16)], F32, rows_out, tm_out, _tile(d, 1024, 2 * LANES),
                        tk=_tile(d_ff, 2048, 2 * LANES), name="ffn_down")
        else:
            d_e = moe_w_gate.shape[3]
            w_r = jnp.pad(router_w[j], ((0, 0), (0, LANES - n_experts))).astype(BF16)
            comb, sel = _router(u2, w_r, rows_out, n_experts)
            rank, counts = _rank(sel)
            pos, wts = _slots(rank, sel, comb, counts)
            pos_a, pos_b = pos[:, 0], pos[:, 1]
            n_sorted = TOP_K * rows_out + n_experts * MOE_TILE
            tiles_per_expert = (counts[0, :n_experts].astype(jnp.int32) + MOE_TILE - 1) // MOE_TILE
            tile_end = jnp.cumsum(tiles_per_expert)
            tile_expert = jnp.minimum(
                jnp.sum(jnp.arange(n_sorted // MOE_TILE)[:, None] >= tile_end[None, :], axis=1), n_experts - 1
            ).astype(jnp.int32)
            xs = _dispatch(_invert(pos_a, pos_b, n_sorted), u2)
            hs = _expert_glu(tile_expert, xs, moe_w_gate, moe_w_up, j, _tile(d_e, 512, 2 * LANES))
            os = _expert_down(tile_expert, hs, moe_w_down, j, _tile(d, 1024, 2 * LANES))

        if moe:
            assert last
            h = _combine_post(pos_a, pos_b, os, h, wts, gv(layer, 3), gt2, group, rows_out)
        elif last:
            (h,) = _post(h, f, gv(layer, 3), gt2, group, rows_out)
        else:
            nsh1, nsc1 = mods(layer + 1)[:2]
            h, u = _post(h, f, gv(layer, 3), gt2, group, rows_out, nxt=(gv(layer + 1, 0), nsh1, nsc1))

    return h[:t_lat].reshape(n_batch, seq, d)
```
